```python
import math
import jax, jax.numpy as jnp
from jax import lax
import numpy as np

D_MODEL = 1024
BATCH = 8
SEQ = 2048
DEPTH = 2

CHUNK = 64
HEAD_DIM = 64
N_GROUPS = 4
HEADS_PER_GROUP = D_MODEL // (N_GROUPS * HEAD_DIM)
GROUP_W = HEADS_PER_GROUP * HEAD_DIM
MIX_W = N_GROUPS * GROUP_W
A_LEFT_CHUNKS = 8
REL_CLIP = 128
IDX_HEADS = 8
IDX_DIM = 32
TOPK_CAP = 256
DIFF_DIM = HEAD_DIM // 2
Q_BLOCK = 128
ROPE_THETA = 10000.0
D_FF = -(-8 * D_MODEL // (3 * 256)) * 256
EPS = 1e-6

IN_WIDTHS = (3 * GROUP_W,
             3 * GROUP_W,
             IDX_HEADS * IDX_DIM,
             IDX_DIM,
             IDX_HEADS,
             3 * GROUP_W,
             HEADS_PER_GROUP,
             3 * GROUP_W)
IN_W = sum(IN_WIDTHS)
IN_SPLIT_OFFSETS = tuple(int(o) for o in np.cumsum(IN_WIDTHS)[:-1])

kernel_name = "hybrid_chunk_causal_parallel_heads"


def rmsnorm(x, g):
    xf = x.astype(jnp.float32)
    y = xf * lax.rsqrt(jnp.mean(xf * xf, axis=-1, keepdims=True) + EPS)
    return (y * g.astype(jnp.float32)).astype(x.dtype)


def rope(x, pos):
    d = x.shape[-1]
    inv = ROPE_THETA ** (-jnp.arange(0, d, 2, dtype=jnp.float32) / d)
    ang = pos.astype(jnp.float32)[:, None] * inv[None, :]
    cos = jnp.cos(ang)[:, None, :]
    sin = jnp.sin(ang)[:, None, :]
    xf = x.astype(jnp.float32)
    x1, x2 = xf[..., : d // 2], xf[..., d // 2:]
    return jnp.concatenate([x1 * cos - x2 * sin, x2 * cos + x1 * sin], axis=-1).astype(x.dtype)


def sweep_query_blocks(fn, *qs):
    B, S = qs[0].shape[:2]
    nb = S // Q_BLOCK
    blocks = tuple(jnp.swapaxes(a.reshape(B, nb, Q_BLOCK, *a.shape[2:]), 0, 1) for a in qs)
    out = lax.map(lambda xs: fn(xs[0], *xs[1:]), (jnp.arange(nb), *blocks))
    out = jnp.swapaxes(out, 0, 1)
    return out.reshape(B, S, *out.shape[3:])


def chunk_band_attention(q, k, v, rel_bias):
    B, S, H, d = q.shape
    nc = S // CHUNK
    nband = A_LEFT_CHUNKS + 1
    qc = q.reshape(B, nc, CHUNK, H, d)

    def band(a):
        ac = a.reshape(B, nc, CHUNK, H, d)
        ap = jnp.pad(ac, ((0, 0), (A_LEFT_CHUNKS, 0), (0, 0), (0, 0), (0, 0)))
        return jnp.concatenate([ap[:, j:j + nc] for j in range(nband)], axis=2)

    kb, vb = band(k), band(v)
    s = jnp.einsum('bcihd,bcjhd->bhcij', qc, kb).astype(jnp.float32) * (d ** -0.5)
    i_pos = jnp.arange(CHUNK)[:, None]
    p_pos = jnp.arange(nband * CHUNK)[None, :]
    rel = A_LEFT_CHUNKS * CHUNK + i_pos - p_pos
    bias = rel_bias[:, jnp.clip(rel, -REL_CLIP, REL_CLIP) + REL_CLIP].astype(jnp.float32)
    key_chunk = jnp.arange(nc)[:, None] - A_LEFT_CHUNKS + p_pos // CHUNK
    valid = (key_chunk >= 0)[None, None, :, None, :]
    s = jnp.where(valid, s + bias[None, :, None], -jnp.inf)
    p = jax.nn.softmax(s, axis=-1).astype(v.dtype)
    o = jnp.einsum('bhcij,bcjhd->bcihd', p, vb)
    return o.reshape(B, S, H * d)


def dsa_sparse_attention(q, k, v, q_idx, k_idx, w_idx):
    B, S, H, d = q.shape
    n_sel = min(TOPK_CAP, S // 4)
    key_chunk = jnp.arange(S) // CHUNK

    def block(bi, qb, qib, wb):
        q_chunk = (bi * Q_BLOCK + jnp.arange(Q_BLOCK)) // CHUNK
        admiss = key_chunk[None, :] <= q_chunk[:, None]
        sc = jnp.einsum('bqgd,bsd->bqgs', qib, k_idx).astype(jnp.float32) * (IDX_DIM ** -0.5)
        score = jnp.einsum('bqg,bqgs->bqs', wb.astype(jnp.float32) * (IDX_HEADS ** -0.5), jax.nn.relu(sc))
        score = jnp.where(admiss[None], score, -jnp.inf)
        _, sel = lax.top_k(score, n_sel)
        sel_ok = key_chunk[sel] <= q_chunk[None, :, None]
        ks = jax.vmap(lambda kk, ii: kk[ii])(k, sel)
        vs = jax.vmap(lambda vv, ii: vv[ii])(v, sel)
        s = jnp.einsum('bqhd,bqkhd->bhqk', qb, ks).astype(jnp.float32) * (d ** -0.5)
        s = jnp.where(sel_ok[:, None], s, -jnp.inf)
        p = jax.nn.softmax(s, axis=-1).astype(v.dtype)
        return jnp.einsum('bhqk,bqkhd->bqhd', p, vs)

    o = sweep_query_blocks(block, q, q_idx, w_idx)
    return o.reshape(B, S, H * d)


def forgetting_attention(q, k, v, f_logit):
    B, S, H, d = q.shape
    cum = jnp.cumsum(jax.nn.log_sigmoid(f_logit.astype(jnp.float32)), axis=1)
    cum_keys = jnp.swapaxes(cum, 1, 2)
    key_pos = jnp.arange(S)

    def block(bi, qb, cb):
        q_pos = bi * Q_BLOCK + jnp.arange(Q_BLOCK)
        s = jnp.einsum('bqhd,bshd->bhqs', qb, k).astype(jnp.float32) * (d ** -0.5)
        s = s + jnp.swapaxes(cb, 1, 2)[..., None] - cum_keys[:, :, None, :]
        s = jnp.where((key_pos[None, :] <= q_pos[:, None])[None, None], s, -jnp.inf)
        p = jax.nn.softmax(s, axis=-1).astype(v.dtype)
        return jnp.einsum('bhqs,bshd->bqhd', p, v)

    o = sweep_query_blocks(block, q, cum)
    return o.reshape(B, S, H * d)


def differential_attention(q, k, v, lam, subln_g, lam_init):
    B, S, H, d = v.shape
    key_chunk = jnp.arange(S) // CHUNK

    def block(bi, qb):
        q_chunk = (bi * Q_BLOCK + jnp.arange(Q_BLOCK)) // CHUNK
        mask = key_chunk[None, :] <= q_chunk[:, None]
        s = jnp.einsum('bqhrd,bshrd->bhrqs', qb, k).astype(jnp.float32) * (DIFF_DIM ** -0.5)
        s = jnp.where(mask[None, None, None], s, -jnp.inf)
        p = jax.nn.softmax(s, axis=-1)
        a = (p[:, :, 0] - lam * p[:, :, 1]).astype(v.dtype)
        return jnp.einsum('bhqs,bshd->bqhd', a, v)

    o = sweep_query_blocks(block, q)
    o = rmsnorm(o, subln_g) * (1.0 - lam_init)
    return o.reshape(B, S, H * d)


def setup_inputs(seed: int = 0) -> dict:
    key = jax.random.key(seed)
    ks = jax.random.split(key, 16)
    f32 = jnp.float32
    L = DEPTH
    nrm = lambda k, shape, scale: jax.random.normal(k, shape, f32) * scale
    return {
        "x": nrm(ks[0], (BATCH, SEQ, D_MODEL), 1.0),
        "ln1_g": 1.0 + nrm(ks[1], (L, D_MODEL), 0.02),
        "w_in": nrm(ks[2], (L, D_MODEL, IN_W), D_MODEL ** -0.5),
        "rel_bias": nrm(ks[3], (L, HEADS_PER_GROUP, 2 * REL_CLIP + 1), 0.1),
        "forget_b": 1.0 + 2.0 * jax.random.uniform(ks[4], (L, HEADS_PER_GROUP), f32),
        "lam_q1": nrm(ks[5], (L, DIFF_DIM), 0.1),
        "lam_k1": nrm(ks[6], (L, DIFF_DIM), 0.1),
        "lam_q2": nrm(ks[7], (L, DIFF_DIM), 0.1),
        "lam_k2": nrm(ks[8], (L, DIFF_DIM), 0.1),
        "diff_norm_g": 1.0 + nrm(ks[9], (L, HEAD_DIM), 0.02),
        "w_o": nrm(ks[10], (L, MIX_W, D_MODEL), MIX_W ** -0.5),
        "ln2_g": 1.0 + nrm(ks[11], (L, D_MODEL), 0.02),
        "w_gate": nrm(ks[12], (L, D_MODEL, D_FF), D_MODEL ** -0.5),
        "w_up": nrm(ks[13], (L, D_MODEL, D_FF), D_MODEL ** -0.5),
        "w_down": nrm(ks[14], (L, D_FF, D_MODEL), D_FF ** -0.5),
        "final_g": 1.0 + nrm(ks[15], (D_MODEL,), 0.02),
    }


def reference(x, ln1_g, w_in, rel_bias, forget_b, lam_q1, lam_k1, lam_q2, lam_k2,
              diff_norm_g, w_o, ln2_g, w_gate, w_up, w_down, final_g):
    B, S, _ = x.shape
    H, d = HEADS_PER_GROUP, HEAD_DIM
    pos = jnp.arange(S)
    for l in range(DEPTH):
        lam_init = 0.8 - 0.6 * math.exp(-0.3 * l)
        h = rmsnorm(x, ln1_g[l])
        proj = jnp.einsum('bsd,de->bse', h, w_in[l])
        a_qkv, b_qkv, qi, ki, wi, c_qkv, fg, d_qkv = jnp.split(proj, IN_SPLIT_OFFSETS, axis=-1)

        aq, ak, av = [t[:, :, 0] for t in jnp.split(a_qkv.reshape(B, S, 3, H, d), 3, axis=2)]
        out_a = chunk_band_attention(aq, ak, av, rel_bias[l])

        bq, bk, bv = [t[:, :, 0] for t in jnp.split(b_qkv.reshape(B, S, 3, H, d), 3, axis=2)]
        q_idx = rope(qi.reshape(B, S, IDX_HEADS, IDX_DIM), pos)
        k_idx = rope(ki.reshape(B, S, 1, IDX_DIM), pos)[:, :, 0]
        out_b = dsa_sparse_attention(rope(bq, pos), rope(bk, pos), bv, q_idx, k_idx, wi)

        cq, ck, cv = [t[:, :, 0] for t in jnp.split(c_qkv.reshape(B, S, 3, H, d), 3, axis=2)]
        out_c = forgetting_attention(cq, ck, cv, fg + forget_b[l])

        dq, dk, dv = jnp.split(d_qkv, 3, axis=-1)
        dq = rope(dq.reshape(B, S, 2 * H, DIFF_DIM), pos).reshape(B, S, H, 2, DIFF_DIM)
        dk = rope(dk.reshape(B, S, 2 * H, DIFF_DIM), pos).reshape(B, S, H, 2, DIFF_DIM)
        lam = (jnp.exp(jnp.sum(lam_q1[l].astype(jnp.float32) * lam_k1[l].astype(jnp.float32)))
               - jnp.exp(jnp.sum(lam_q2[l].astype(jnp.float32) * lam_k2[l].astype(jnp.float32)))
               + lam_init)
        out_d = differential_attention(dq, dk, dv.reshape(B, S, H, d), lam, diff_norm_g[l], lam_init)

        mixed = jnp.concatenate([out_a, out_b, out_c, out_d], axis=-1)
        x = x + jnp.einsum('bse,ed->bsd', mixed, w_o[l])

        h2 = rmsnorm(x, ln2_g[l])
        gate = jax.nn.silu(jnp.einsum('bsd,df->bsf', h2, w_gate[l]))
        up = jnp.einsum('bsd,df->bsf', h2, w_up[l])
        x = x + jnp.einsum('bsf,fd->bsd', gate * up, w_down[l])
    return rmsnorm(x, final_g)
```

```python
import functools
import math

import jax
import jax.numpy as jnp
import numpy as np
from jax import lax
from jax.experimental import pallas as pl
from jax.experimental.pallas import tpu as pltpu

D_MODEL = 1024
SEQ = 2048
DEPTH = 2
CHUNK = 64
HEAD_DIM = 64
HEADS = 4
GROUP_W = HEADS * HEAD_DIM
A_LEFT_CHUNKS = 8
REL_CLIP = 128
IDX_HEADS = 8
IDX_DIM = 32
TOP_K = 256
DIFF_DIM = 32
ROPE_THETA = 10000.0
D_FF = 2816
EPS = 1e-6

OFF_A = 0
OFF_B = 3 * GROUP_W
OFF_IQ = 6 * GROUP_W
OFF_C = 7 * GROUP_W
OFF_D = 10 * GROUP_W
OFF_SMALL = 13 * GROUP_W
SMALL_W = 128
PROJ_W = OFF_SMALL + SMALL_W
SM_KI = 0
SM_WI = IDX_DIM
SM_FG = IDX_DIM + IDX_HEADS

VMEM_LIMIT = 56 * 1024 * 1024

TM_PROJ = 512
TM_FFN = 512
TQ_A = 2 * CHUNK
WIN_A = (A_LEFT_CHUNKS + 2) * CHUNK
PAD_A = A_LEFT_CHUNKS * CHUNK
TQ_B = 128
TK_B = 128
SB_B = 256
TQ_C = 256
TQ_D = 256

F32 = jnp.float32
BF16 = jnp.bfloat16
NEG_INF = float("-inf")
INT_MIN = -2 ** 31
NEG_INF_KEY = int(np.int32(np.uint32(0xFF800000) ^ np.uint32(0x7FFFFFFF)))


def _dot(a, b):
    return jnp.dot(a, b, preferred_element_type=F32)


def _dot_nt(a, b):
    return lax.dot_general(a, b, (((1,), (1,)), ((), ())), preferred_element_type=F32)


def _rmsnorm(x, g):
    return x * lax.rsqrt(jnp.mean(x * x, axis=-1, keepdims=True) + EPS) * g


def _rope(x, cos, sin_signed, half):
    n = x.shape[-1]
    lane = lax.broadcasted_iota(jnp.int32, x.shape, 1)
    first = (lane & half) == 0
    swapped = jnp.where(first, pltpu.roll(x, n - half, 1), pltpu.roll(x, half, 1))
    return x * cos + swapped * sin_signed


def _proj_body(x_ref, g_ref, w_ref, cos64_ref, sin64_ref, cos32_ref, sin32_ref,
               coss_ref, sins_ref, fb_ref,
               aq_ref, ak_ref, av_ref, bq_ref, bk_ref, bv_ref, iq_ref,
               cq_ref, ck_ref, cv_ref, dq_ref, dk_ref, dv_ref, small_ref, cum_ref,
               carry_ref):
    i = pl.program_id(1)
    tm = x_ref.shape[1]
    h = _rmsnorm(x_ref[0], g_ref[...]).astype(BF16)
    proj = _dot(h, w_ref[...])

    def grp(off, j):
        return proj[:, off + j * GROUP_W: off + (j + 1) * GROUP_W]

    scale = HEAD_DIM ** -0.5
    aq_ref[0] = (grp(OFF_A, 0) * scale).astype(BF16)
    ak_ref[0] = grp(OFF_A, 1).astype(BF16)
    av_ref[0] = grp(OFF_A, 2).astype(BF16)

    cos64, sin64 = cos64_ref[...], sin64_ref[...]
    bq_ref[0] = (_rope(grp(OFF_B, 0), cos64, sin64, HEAD_DIM // 2) * scale).astype(BF16)
    bk_ref[0] = _rope(grp(OFF_B, 1), cos64, sin64, HEAD_DIM // 2).astype(BF16)
    bv_ref[0] = grp(OFF_B, 2).astype(BF16)

    cos32, sin32 = cos32_ref[...], sin32_ref[...]
    iq_ref[0] = _rope(grp(OFF_IQ, 0), cos32, sin32, IDX_DIM // 2).astype(BF16)

    cq_ref[0] = (grp(OFF_C, 0) * scale).astype(BF16)
    ck_ref[0] = grp(OFF_C, 1).astype(BF16)
    cv_ref[0] = grp(OFF_C, 2).astype(BF16)

    dq_ref[0] = _rope(grp(OFF_D, 0), cos32, sin32, DIFF_DIM // 2).astype(BF16)
    dk_ref[0] = _rope(grp(OFF_D, 1), cos32, sin32, DIFF_DIM // 2).astype(BF16)
    dv_ref[0] = grp(OFF_D, 2).astype(BF16)

    small = proj[:, OFF_SMALL:OFF_SMALL + SMALL_W]
    small_ref[0] = _rope(small, coss_ref[...], sins_ref[...], IDX_DIM // 2)

    lane = lax.broadcasted_iota(jnp.int32, small.shape, 1)
    is_fg = (lane >= SM_FG) & (lane < SM_FG + HEADS)
    logf = jnp.where(is_fg, jax.nn.log_sigmoid(small + fb_ref[...]), 0.0)
    row = lax.broadcasted_iota(jnp.int32, (tm, tm), 0)
    col = lax.broadcasted_iota(jnp.int32, (tm, tm), 1)
    tri = jnp.where(row >= col, 1.0, 0.0).astype(F32)

    @pl.when(i == 0)
    def _():
        carry_ref[...] = jnp.zeros_like(carry_ref)

    cum = jnp.dot(tri, logf, precision=lax.Precision.HIGHEST,
                  preferred_element_type=F32) + carry_ref[...]
    carry_ref[...] = cum[tm - 1:tm, :]
    cum_ref[0] = cum


def _proj_call(x, g, w, tabs, fb):
    B, S, _ = x.shape
    tm = TM_PROJ
    nt = S // tm
    cos64, sin64, cos32, sin32, coss, sins = tabs
    tok = lambda w_: pl.BlockSpec((1, tm, w_), lambda b, i: (b, i, 0))
    tab = lambda w_: pl.BlockSpec((tm, w_), lambda b, i: (i, 0))
    const = lambda shape: pl.BlockSpec(shape, lambda b, i: (0,) * len(shape),
                                       pipeline_mode=pl.Buffered(1))
    bf = lambda: jax.ShapeDtypeStruct((B, S, GROUP_W), BF16)
    out_shape = [bf() for _ in range(13)] + [
        jax.ShapeDtypeStruct((B, S, SMALL_W), F32),
        jax.ShapeDtypeStruct((B, S, SMALL_W), F32)]
    out_specs = [tok(GROUP_W) for _ in range(13)] + [tok(SMALL_W), tok(SMALL_W)]
    return pl.pallas_call(
        _proj_body,
        name="proj",
        grid=(B, nt),
        in_specs=[tok(D_MODEL), const((1, D_MODEL)), const((D_MODEL, PROJ_W)),
                  tab(GROUP_W), tab(GROUP_W), tab(GROUP_W), tab(GROUP_W),
                  tab(SMALL_W), tab(SMALL_W), const((1, SMALL_W))],
        out_specs=out_specs,
        out_shape=out_shape,
        scratch_shapes=[pltpu.VMEM((1, SMALL_W), F32)],
        compiler_params=pltpu.CompilerParams(
            dimension_semantics=("arbitrary", "arbitrary"),
            vmem_limit_bytes=VMEM_LIMIT),
    )(x, g, w, cos64, sin64, cos32, sin32, coss, sins, fb)


def _online_update(s, v, carry):
    m, l, acc = carry
    m_new = jnp.maximum(m, jnp.max(s, axis=1, keepdims=True))
    m_safe = jnp.where(m_new == NEG_INF, 0.0, m_new)
    alpha = jnp.exp(m - m_safe)
    p = jnp.exp(s - m_safe)
    l = alpha * l + jnp.sum(p, axis=1, keepdims=True)
    acc = alpha * acc + _dot(p.astype(BF16), v)
    return m_new, l, acc


def _online_init(tq, d):
    return (jnp.full((tq, 1), NEG_INF, F32), jnp.zeros((tq, 1), F32), jnp.zeros((tq, d), F32))


def _band_body(q_ref, k_ref, v_ref, bias_ref, o_ref, kpad_ref, vpad_ref):
    qt = pl.program_id(1)
    S = k_ref.shape[1]

    @pl.when(qt == 0)
    def _():
        kpad_ref[0:PAD_A, :] = jnp.zeros((PAD_A, GROUP_W), BF16)
        vpad_ref[0:PAD_A, :] = jnp.zeros((PAD_A, GROUP_W), BF16)
        kpad_ref[PAD_A:PAD_A + S, :] = k_ref[0]
        vpad_ref[PAD_A:PAD_A + S, :] = v_ref[0]

    start = pl.multiple_of(qt * TQ_A, TQ_A)
    kw = kpad_ref[pl.ds(start, WIN_A), :]
    vw = vpad_ref[pl.ds(start, WIN_A), :]
    q = q_ref[0]
    col = lax.broadcasted_iota(jnp.int32, (TQ_A, WIN_A), 1)
    real_key = col >= PAD_A - qt * TQ_A
    outs = []
    for h in range(HEADS):
        sl = slice(h * HEAD_DIM, (h + 1) * HEAD_DIM)
        s = _dot_nt(q[:, sl], kw[:, sl]) + bias_ref[h]
        s = jnp.where(real_key, s, NEG_INF)
        m = jnp.max(s, axis=1, keepdims=True)
        p = jnp.exp(s - m)
        l = jnp.sum(p, axis=1, keepdims=True)
        outs.append(_dot(p.astype(BF16), vw[:, sl]) / l)
    o_ref[0] = jnp.concatenate(outs, axis=1).astype(BF16)


def _band_call(q, k, v, bias):
    B, S, _ = q.shape
    return pl.pallas_call(
        _band_body,
        name="band",
        grid=(B, S // TQ_A),
        in_specs=[pl.BlockSpec((1, TQ_A, GROUP_W), lambda b, i: (b, i, 0)),
                  pl.BlockSpec((1, S, GROUP_W), lambda b, i: (b, 0, 0)),
                  pl.BlockSpec((1, S, GROUP_W), lambda b, i: (b, 0, 0)),
                  pl.BlockSpec((HEADS, TQ_A, WIN_A), lambda b, i: (0, 0, 0))],
        out_specs=pl.BlockSpec((1, TQ_A, GROUP_W), lambda b, i: (b, i, 0)),
        out_shape=jax.ShapeDtypeStruct((B, S, GROUP_W), BF16),
        scratch_shapes=[pltpu.VMEM((PAD_A + S, GROUP_W), BF16),
                        pltpu.VMEM((PAD_A + S, GROUP_W), BF16)],
        compiler_params=pltpu.CompilerParams(
            dimension_semantics=("arbitrary", "arbitrary"),
            vmem_limit_bytes=VMEM_LIMIT),
    )(q, k, v, bias)


def _band_bias(rel_bias):
    r = np.arange(TQ_A)[:, None]
    j = np.arange(WIN_A)[None, :]
    rel = A_LEFT_CHUNKS * CHUNK + r - j
    idx = np.clip(rel, -REL_CLIP, REL_CLIP) + REL_CLIP
    dchunk = r // CHUNK + A_LEFT_CHUNKS - j // CHUNK
    in_band = (dchunk >= 0) & (dchunk <= A_LEFT_CHUNKS)
    bias = rel_bias.astype(F32)[:, idx]
    return jnp.where(jnp.asarray(in_band)[None], bias, NEG_INF)


def _sparse_body(q_ref, k_ref, v_ref, iq_ref, small_q_ref, small_k_ref, o_ref, key_ref):
    qt = pl.program_id(1)
    tq = TQ_B
    n_kb = qt + 1
    n_sb = (qt + 2) // 2

    small_q = small_q_ref[0]
    w_idx = small_q[:, SM_WI:SM_WI + IDX_HEADS] * ((IDX_DIM * IDX_HEADS) ** -0.5)
    iq = iq_ref[0]
    row = lax.broadcasted_iota(jnp.int32, (tq, TK_B), 0)
    col = lax.broadcasted_iota(jnp.int32, (tq, TK_B), 1)
    diag_ok = (col // CHUNK) <= (row // CHUNK)

    def score_block(kb, _):
        off = pl.multiple_of(kb * TK_B, TK_B)
        ki = small_k_ref[0, pl.ds(off, TK_B), :][:, SM_KI:SM_KI + IDX_DIM].astype(BF16)
        sc = jnp.zeros((tq, TK_B), F32)
        for g in range(IDX_HEADS):
            sg = _dot_nt(iq[:, g * IDX_DIM:(g + 1) * IDX_DIM], ki)
            sc = sc + w_idx[:, g:g + 1] * jnp.maximum(sg, 0.0)
        sc = jnp.where(sc == 0.0, 0.0, sc)
        sc = jnp.where((kb < qt) | diag_ok, sc, NEG_INF)
        bits = pltpu.bitcast(sc, jnp.int32)
        key_ref[:, pl.ds(off, TK_B)] = jnp.where(bits < 0, bits ^ 0x7FFFFFFF, bits)
        return 0

    lax.fori_loop(0, n_kb, score_block, 0)

    @pl.when(qt % 2 == 0)
    def _():
        off = pl.multiple_of((qt + 1) * TK_B, TK_B)
        key_ref[:, pl.ds(off, TK_B)] = jnp.full((tq, TK_B), NEG_INF_KEY, jnp.int32)

    def count_ge(cand):
        def body(sb, cnt):
            off = pl.multiple_of(sb * SB_B, SB_B)
            blk = key_ref[:, pl.ds(off, SB_B)]
            c = jnp.where(blk >= cand, 1.0, 0.0)
            return cnt + c[:, :128] + c[:, 128:]
        cnt = lax.fori_loop(0, n_sb, body, jnp.zeros((tq, 128), F32))
        return jnp.sum(cnt, axis=1, keepdims=True)

    t0 = jnp.where(count_ge(jnp.zeros((tq, 1), jnp.int32)) >= TOP_K, 0, INT_MIN).astype(jnp.int32)

    def search(it, t):
        cand = t | jnp.left_shift(jnp.int32(1), 30 - it)
        return jnp.where(count_ge(cand) >= TOP_K, cand, t)

    thr = lax.fori_loop(0, 31, search, t0)

    def count_gt_body(sb, cnt):
        off = pl.multiple_of(sb * SB_B, SB_B)
        blk = key_ref[:, pl.ds(off, SB_B)]
        c = jnp.where(blk > thr, 1.0, 0.0)
        return cnt + c[:, :128] + c[:, 128:]
    n_gt = jnp.sum(lax.fori_loop(0, n_sb, count_gt_body, jnp.zeros((tq, 128), F32)),
                   axis=1, keepdims=True)
    need = jnp.where(thr <= NEG_INF_KEY, 0.0, TOP_K - n_gt)

    tr = lax.broadcasted_iota(jnp.int32, (TK_B, TK_B), 0)
    tc = lax.broadcasted_iota(jnp.int32, (TK_B, TK_B), 1)
    before = jnp.where(tr < tc, 1.0, 0.0).astype(BF16)
    q = q_ref[0]

    def attend_block(kb, carry):
        ties_seen, heads = carry
        off = pl.multiple_of(kb * TK_B, TK_B)
        blk = key_ref[:, pl.ds(off, TK_B)]
        tie = blk == thr
        tie_f = jnp.where(tie, 1.0, 0.0)
        rank = _dot(tie_f.astype(BF16), before) + ties_seen
        sel = (blk > thr) | (tie & (rank < need))
        ties_seen = ties_seen + jnp.sum(tie_f, axis=1, keepdims=True)
        kblk = k_ref[0, pl.ds(off, TK_B), :]
        vblk = v_ref[0, pl.ds(off, TK_B), :]
        new_heads = []
        for h in range(HEADS):
            sl = slice(h * HEAD_DIM, (h + 1) * HEAD_DIM)
            s = jnp.where(sel, _dot_nt(q[:, sl], kblk[:, sl]), NEG_INF)
            new_heads.append(_online_update(s, vblk[:, sl], heads[h]))
        return ties_seen, tuple(new_heads)

    init = (jnp.zeros((tq, 1), F32), tuple(_online_init(tq, HEAD_DIM) for _ in range(HEADS)))
    _, heads = lax.fori_loop(0, n_kb, attend_block, init)
    o_ref[0] = jnp.concatenate([acc / l for (_, l, acc) in heads], axis=1).astype(BF16)


def _sparse_call(q, k, v, iq, small):
    B, S, _ = q.shape
    qspec = lambda w_: pl.BlockSpec((1, TQ_B, w_), lambda b, i: (b, i, 0))
    full = lambda w_: pl.BlockSpec((1, S, w_), lambda b, i: (b, 0, 0))
    return pl.pallas_call(
        _sparse_body,
        name="sparse",
        grid=(B, S // TQ_B),
        in_specs=[qspec(GROUP_W), full(GROUP_W), full(GROUP_W), qspec(GROUP_W),
                  qspec(SMALL_W), full(SMALL_W)],
        out_specs=qspec(GROUP_W),
        out_shape=jax.ShapeDtypeStruct((B, S, GROUP_W), BF16),
        scratch_shapes=[pltpu.VMEM((TQ_B, S), jnp.int32)],
        compiler_params=pltpu.CompilerParams(
            dimension_semantics=("arbitrary", "arbitrary"),
            vmem_limit_bytes=VMEM_LIMIT),
    )(q, k, v, iq, small, small)


def _forget_body(q_ref, k_ref, v_ref, cumt_ref, o_ref):
    qt = pl.program_id(1)
    tq = TQ_C
    q = q_ref[0]
    row = lax.broadcasted_iota(jnp.int32, (tq, tq), 0)
    col = lax.broadcasted_iota(jnp.int32, (tq, tq), 1)
    causal = col <= row
    outs = []
    for h in range(HEADS):
        sl = slice(h * HEAD_DIM, (h + 1) * HEAD_DIM)
        qh = q[:, sl]

        def block(kb, carry, diag):
            off = pl.multiple_of(kb * tq, tq)
            s = _dot_nt(qh, k_ref[0, pl.ds(off, tq), sl]) - cumt_ref[0, h:h + 1, pl.ds(off, tq)]
            if diag:
                s = jnp.where(causal, s, NEG_INF)
            return _online_update(s, v_ref[0, pl.ds(off, tq), sl], carry)

        carry = lax.fori_loop(0, qt, functools.partial(block, diag=False), _online_init(tq, HEAD_DIM))
        _, l, acc = block(qt, carry, True)
        outs.append(acc / l)
    o_ref[0] = jnp.concatenate(outs, axis=1).astype(BF16)


def _forget_call(q, k, v, cum_t):
    B, S, _ = q.shape
    return pl.pallas_call(
        _forget_body,
        name="forget",
        grid=(B, S // TQ_C),
        in_specs=[pl.BlockSpec((1, TQ_C, GROUP_W), lambda b, i: (b, i, 0)),
                  pl.BlockSpec((1, S, GROUP_W), lambda b, i: (b, 0, 0)),
                  pl.BlockSpec((1, S, GROUP_W), lambda b, i: (b, 0, 0)),
                  pl.BlockSpec((1, HEADS, S), lambda b, i: (b, 0, 0))],
        out_specs=pl.BlockSpec((1, TQ_C, GROUP_W), lambda b, i: (b, i, 0)),
        out_shape=jax.ShapeDtypeStruct((B, S, GROUP_W), BF16),
        compiler_params=pltpu.CompilerParams(
            dimension_semantics=("arbitrary", "arbitrary"),
            vmem_limit_bytes=VMEM_LIMIT),
    )(q, k, v, cum_t)


def _diff_body(q_ref, k_ref, v_ref, lq1_ref, lk1_ref, lq2_ref, lk2_ref, g_ref, o_ref, *, lam_init):
    qt = pl.program_id(1)
    tq = TQ_D
    lam = (jnp.exp(jnp.sum(lq1_ref[...] * lk1_ref[...], axis=1, keepdims=True))
           - jnp.exp(jnp.sum(lq2_ref[...] * lk2_ref[...], axis=1, keepdims=True)) + lam_init)
    q = q_ref[0]
    row = lax.broadcasted_iota(jnp.int32, (tq, tq), 0)
    col = lax.broadcasted_iota(jnp.int32, (tq, tq), 1)
    chunk_ok = (col // CHUNK) <= (row // CHUNK)
    scale = DIFF_DIM ** -0.5
    outs = []
    for h in range(HEADS):
        sl = slice(h * HEAD_DIM, (h + 1) * HEAD_DIM)
        halves = [slice(h * HEAD_DIM + r * DIFF_DIM, h * HEAD_DIM + (r + 1) * DIFF_DIM) for r in range(2)]
        qs = [q[:, hs] for hs in halves]

        def block(kb, carry, diag):
            off = pl.multiple_of(kb * tq, tq)
            vh = v_ref[0, pl.ds(off, tq), sl]
            new = []
            for r in range(2):
                s = _dot_nt(qs[r], k_ref[0, pl.ds(off, tq), halves[r]]) * scale
                if diag:
                    s = jnp.where(chunk_ok, s, NEG_INF)
                new.append(_online_update(s, vh, carry[r]))
            return tuple(new)

        init = (_online_init(tq, HEAD_DIM), _online_init(tq, HEAD_DIM))
        carry = lax.fori_loop(0, qt, functools.partial(block, diag=False), init)
        (_, l1, acc1), (_, l2, acc2) = block(qt, carry, True)
        o = acc1 / l1 - lam * (acc2 / l2)
        outs.append(_rmsnorm(o, g_ref[...]) * (1.0 - lam_init))
    o_ref[0] = jnp.concatenate(outs, axis=1).astype(BF16)


def _diff_call(q, k, v, lq1, lk1, lq2, lk2, g, lam_init):
    B, S, _ = q.shape
    vec = lambda n: pl.BlockSpec((1, n), lambda b, i: (0, 0))
    return pl.pallas_call(
        functools.partial(_diff_body, lam_init=lam_init),
        name="diff",
        grid=(B, S // TQ_D),
        in_specs=[pl.BlockSpec((1, TQ_D, GROUP_W), lambda b, i: (b, i, 0)),
                  pl.BlockSpec((1, S, GROUP_W), lambda b, i: (b, 0, 0)),
                  pl.BlockSpec((1, S, GROUP_W), lambda b, i: (b, 0, 0)),
                  vec(DIFF_DIM), vec(DIFF_DIM), vec(DIFF_DIM), vec(DIFF_DIM), vec(HEAD_DIM)],
        out_specs=pl.BlockSpec((1, TQ_D, GROUP_W), lambda b, i: (b, i, 0)),
        out_shape=jax.ShapeDtypeStruct((B, S, GROUP_W), BF16),
        compiler_params=pltpu.CompilerParams(
            dimension_semantics=("arbitrary", "arbitrary"),
            vmem_limit_bytes=VMEM_LIMIT),
    )(q, k, v, lq1, lk1, lq2, lk2, g)


def _ffn_body(x_ref, ma_ref, mb_ref, mc_ref, md_ref, wo_ref, g2_ref, wg_ref, wu_ref, wd_ref,
              gf_ref, o_ref, *, final):
    mixed = jnp.concatenate([ma_ref[...], mb_ref[...], mc_ref[...], md_ref[...]], axis=1)
    x1 = x_ref[...] + _dot(mixed, wo_ref[...])
    h2 = _rmsnorm(x1, g2_ref[...]).astype(BF16)
    gate = jax.nn.silu(_dot(h2, wg_ref[...]))
    up = _dot(h2, wu_ref[...])
    x2 = x1 + _dot((gate * up).astype(BF16), wd_ref[...])
    if final:
        x2 = _rmsnorm(x2, gf_ref[...])
    o_ref[...] = x2


def _ffn_call(x, mixed, wo, g2, wg, wu, wd, gf, final):
    T = x.shape[0]
    tm = TM_FFN
    tok = lambda w_: pl.BlockSpec((tm, w_), lambda i: (i, 0))
    const = lambda shape: pl.BlockSpec(shape, lambda i: (0, 0), pipeline_mode=pl.Buffered(1))
    return pl.pallas_call(
        functools.partial(_ffn_body, final=final),
        name="ffn",
        grid=(T // tm,),
        in_specs=[tok(D_MODEL)] + [tok(GROUP_W)] * 4 + [
            const((D_MODEL, D_MODEL)), const((1, D_MODEL)), const((D_MODEL, D_FF)),
            const((D_MODEL, D_FF)), const((D_FF, D_MODEL)), const((1, D_MODEL))],
        out_specs=tok(D_MODEL),
        out_shape=jax.ShapeDtypeStruct((T, D_MODEL), F32),
        compiler_params=pltpu.CompilerParams(
            dimension_semantics=("arbitrary",),
            vmem_limit_bytes=VMEM_LIMIT),
    )(x, *mixed, wo, g2, wg, wu, wd, gf)


def _rope_tables(S):
    pos = jnp.arange(S, dtype=F32)[:, None]

    def table(d, reps):
        inv = ROPE_THETA ** (-jnp.arange(0, d, 2, dtype=F32) / d)
        ang = pos * inv[None, :]
        cos, sin = jnp.cos(ang), jnp.sin(ang)
        return (jnp.tile(jnp.concatenate([cos, cos], axis=1), (1, reps)),
                jnp.tile(jnp.concatenate([-sin, sin], axis=1), (1, reps)))

    cos64, sin64 = table(HEAD_DIM, HEADS)
    cos32, sin32 = table(IDX_DIM, GROUP_W // IDX_DIM)
    coss = jnp.ones((S, SMALL_W), F32).at[:, :IDX_DIM].set(cos32[:, :IDX_DIM])
    sins = jnp.zeros((S, SMALL_W), F32).at[:, :IDX_DIM].set(sin32[:, :IDX_DIM])
    return cos64, sin64, cos32, sin32, coss, sins


def _reorder_w_in(w):
    o = np.cumsum([0, 3 * GROUP_W, 3 * GROUP_W, IDX_HEADS * IDX_DIM, IDX_DIM, IDX_HEADS,
                   3 * GROUP_W, HEADS, 3 * GROUP_W])
    pad = jnp.zeros((w.shape[0], SMALL_W - IDX_DIM - IDX_HEADS - HEADS), w.dtype)
    parts = [w[:, o[0]:o[1]], w[:, o[1]:o[2]], w[:, o[2]:o[3]], w[:, o[5]:o[6]], w[:, o[7]:o[8]],
             w[:, o[3]:o[4]], w[:, o[4]:o[5]], w[:, o[6]:o[7]], pad]
    return jnp.concatenate(parts, axis=1).astype(BF16)


def kernel(x, ln1_g, w_in, rel_bias, forget_b, lam_q1, lam_k1, lam_q2, lam_k2, diff_norm_g,
           w_o, ln2_g, w_gate, w_up, w_down, final_g):
    B, S, D = x.shape
    tabs = _rope_tables(S)
    row = lambda a: a.reshape(1, -1).astype(F32)
    for l in range(DEPTH):
        lam_init = 0.8 - 0.6 * math.exp(-0.3 * l)
        fb = jnp.zeros((1, SMALL_W), F32).at[0, SM_FG:SM_FG + HEADS].set(forget_b[l])
        (aq, ak, av, bq, bk, bv, iq, cq, ck, cv, dq, dk, dv, small, cum) = _proj_call(
            x, row(ln1_g[l]), _reorder_w_in(w_in[l]), tabs, fb)
        out_a = _band_call(aq, ak, av, _band_bias(rel_bias[l]))
        out_b = _sparse_call(bq, bk, bv, iq, small)
        cum_t = jnp.swapaxes(cum[:, :, SM_FG:SM_FG + HEADS], 1, 2)
        out_c = _forget_call(cq, ck, cv, cum_t)
        out_d = _diff_call(dq, dk, dv, row(lam_q1[l]), row(lam_k1[l]), row(lam_q2[l]),
                           row(lam_k2[l]), row(diff_norm_g[l]), lam_init)
        mixed = [o.reshape(B * S, GROUP_W) for o in (out_a, out_b, out_c, out_d)]
        x = _ffn_call(x.reshape(B * S, D), mixed, w_o[l].astype(BF16), row(ln2_g[l]),
                      w_gate[l].astype(BF16), w_up[l].astype(BF16), w_down[l].astype(BF16),
                      row(final_g), final=(l == DEPTH - 1)).reshape(B, S, D)
    return x
```

```python
import functools
import math

import jax
import jax.numpy as jnp
import numpy as np
from jax import lax
from jax.experimental import pallas as pl
from jax.experimental.pallas import tpu as pltpu

D_MODEL = 1024
SEQ = 2048
DEPTH = 2
CHUNK = 64
HEAD_DIM = 64
HEADS = 4
GROUP_W = HEADS * HEAD_DIM
A_LEFT_CHUNKS = 8
REL_CLIP = 128
IDX_HEADS = 8
IDX_DIM = 32
TOP_K = 256
DIFF_DIM = 32
ROPE_THETA = 10000.0
D_FF = 2816
EPS = 1e-6
LANES = 128

OFF_A = 0
OFF_B = 3 * GROUP_W
OFF_IQ = 6 * GROUP_W
OFF_C = 7 * GROUP_W
OFF_D = 10 * GROUP_W
OFF_SMALL = 13 * GROUP_W
SMALL_W = LANES
PROJ_W = OFF_SMALL + SMALL_W
SM_KI = 0
SM_WI = IDX_DIM
SM_FG = IDX_DIM + IDX_HEADS

VMEM_LIMIT = 56 * 1024 * 1024

TM_PROJ = 512
TM_FFN = 512
TQ_A = 2 * CHUNK
WIN_A = (A_LEFT_CHUNKS + 2) * CHUNK
PAD_A = A_LEFT_CHUNKS * CHUNK
BASE_A = WIN_A + TQ_A
TQ_B = 256
TQ_C = 256
TQ_D = 256

F32 = jnp.float32
BF16 = jnp.bfloat16
NEG_INF = float("-inf")
INT_MIN = -2 ** 31
NEG_INF_KEY = int(np.int32(np.uint32(0xFF800000) ^ np.uint32(0x7FFFFFFF)))
LOG2E = math.log2(math.e)


def _dot(a, b):
    return jnp.dot(a, b, preferred_element_type=F32)


def _dot_nt(a, b):
    return lax.dot_general(a, b, (((1,), (1,)), ((), ())), preferred_element_type=F32)


def _rmsnorm(x, g):
    return x * lax.rsqrt(jnp.mean(x * x, axis=-1, keepdims=True) + EPS) * g


def _rope(x, cos, sin_signed, half):
    n = x.shape[-1]
    lane = lax.broadcasted_iota(jnp.int32, x.shape, 1)
    first = (lane & half) == 0
    swapped = jnp.where(first, pltpu.roll(x, n - half, 1), pltpu.roll(x, half, 1))
    return x * cos + swapped * sin_signed


def _proj_body(x_ref, g_ref, w_ref, cos64_ref, sin64_ref, cos32_ref, sin32_ref,
               coss_ref, sins_ref, fb_ref,
               aq_ref, ak_ref, av_ref, bq_ref, bk_ref, bvt_ref, iq_ref,
               cq_ref, ck_ref, cvt_ref, dq_ref, dk_ref, dvt_ref, small_ref, cum_ref,
               carry_ref):
    i = pl.program_id(1)
    tm = x_ref.shape[1]
    h = _rmsnorm(x_ref[0], g_ref[...]).astype(BF16)
    proj = _dot(h, w_ref[...])

    def grp(off, j):
        return proj[:, off + j * GROUP_W: off + (j + 1) * GROUP_W]

    scale = HEAD_DIM ** -0.5
    aq_ref[0] = (grp(OFF_A, 0) * scale).astype(BF16)
    ak_ref[0] = grp(OFF_A, 1).astype(BF16)
    av_ref[0] = grp(OFF_A, 2).astype(BF16)

    cos64, sin64 = cos64_ref[...], sin64_ref[...]
    bq_ref[0] = (_rope(grp(OFF_B, 0), cos64, sin64, HEAD_DIM // 2) * scale).astype(BF16)
    bk_ref[0] = _rope(grp(OFF_B, 1), cos64, sin64, HEAD_DIM // 2).astype(BF16)
    bvt_ref[0] = grp(OFF_B, 2).T.astype(BF16)

    cos32, sin32 = cos32_ref[...], sin32_ref[...]
    iq_ref[0] = _rope(grp(OFF_IQ, 0), cos32, sin32, IDX_DIM // 2).astype(BF16)

    cq_ref[0] = (grp(OFF_C, 0) * scale).astype(BF16)
    ck_ref[0] = grp(OFF_C, 1).astype(BF16)
    cvt_ref[0] = grp(OFF_C, 2).T.astype(BF16)

    dq_ref[0] = _rope(grp(OFF_D, 0), cos32, sin32, DIFF_DIM // 2).astype(BF16)
    dk_ref[0] = _rope(grp(OFF_D, 1), cos32, sin32, DIFF_DIM // 2).astype(BF16)
    dvt_ref[0] = grp(OFF_D, 2).T.astype(BF16)

    small = proj[:, OFF_SMALL:OFF_SMALL + SMALL_W]
    small_ref[0] = _rope(small, coss_ref[...], sins_ref[...], IDX_DIM // 2)

    lane = lax.broadcasted_iota(jnp.int32, small.shape, 1)
    is_fg = (lane >= SM_FG) & (lane < SM_FG + HEADS)
    logf = jnp.where(is_fg, jax.nn.log_sigmoid(small + fb_ref[...]), 0.0)
    row = lax.broadcasted_iota(jnp.int32, (tm, tm), 0)
    col = lax.broadcasted_iota(jnp.int32, (tm, tm), 1)
    tri = jnp.where(row >= col, 1.0, 0.0).astype(F32)

    @pl.when(i == 0)
    def _():
        carry_ref[...] = jnp.zeros_like(carry_ref)

    cum = jnp.dot(tri, logf, precision=lax.Precision.HIGHEST,
                  preferred_element_type=F32) + carry_ref[...]
    carry_ref[...] = cum[tm - 1:tm, :]
    cum_ref[0] = cum


def _proj_call(x, g, w, tabs, fb):
    B, S, _ = x.shape
    tm = TM_PROJ
    nt = S // tm
    cos64, sin64, cos32, sin32, coss, sins = tabs
    tok = lambda w_: pl.BlockSpec((1, tm, w_), lambda b, i: (b, i, 0))
    tok_t = pl.BlockSpec((1, GROUP_W, tm), lambda b, i: (b, 0, i))
    tab = lambda w_: pl.BlockSpec((tm, w_), lambda b, i: (i, 0))
    const = lambda shape: pl.BlockSpec(shape, lambda b, i: (0,) * len(shape),
                                       pipeline_mode=pl.Buffered(1))
    bf = jax.ShapeDtypeStruct((B, S, GROUP_W), BF16)
    bf_t = jax.ShapeDtypeStruct((B, GROUP_W, S), BF16)
    small = jax.ShapeDtypeStruct((B, S, SMALL_W), F32)
    out_shape = [bf, bf, bf, bf, bf, bf_t, bf, bf, bf, bf_t, bf, bf, bf_t, small, small]
    g_, t_ = tok(GROUP_W), tok_t
    out_specs = [g_, g_, g_, g_, g_, t_, g_, g_, g_, t_, g_, g_, t_, tok(SMALL_W), tok(SMALL_W)]
    return pl.pallas_call(
        _proj_body,
        name="proj",
        grid=(B, nt),
        in_specs=[tok(D_MODEL), const((1, D_MODEL)), const((D_MODEL, PROJ_W)),
                  tab(GROUP_W), tab(GROUP_W), tab(GROUP_W), tab(GROUP_W),
                  tab(SMALL_W), tab(SMALL_W), const((1, SMALL_W))],
        out_specs=out_specs,
        out_shape=out_shape,
        scratch_shapes=[pltpu.VMEM((1, SMALL_W), F32)],
        compiler_params=pltpu.CompilerParams(
            dimension_semantics=("arbitrary", "arbitrary"),
            vmem_limit_bytes=VMEM_LIMIT),
    )(x, g, w, cos64, sin64, cos32, sin32, coss, sins, fb)


def _band_body(q_ref, k_ref, v_ref, base_ref, o_ref, kpad_ref, vpad_ref, bias_ref):
    b = pl.program_id(0)
    qt = pl.program_id(1)
    S = k_ref.shape[1]

    @pl.when((b == 0) & (qt == 0))
    def _():
        r = lax.broadcasted_iota(jnp.int32, (TQ_A, WIN_A), 0)
        j = lax.broadcasted_iota(jnp.int32, (TQ_A, WIN_A), 1)
        dchunk = r // CHUNK + A_LEFT_CHUNKS - j // CHUNK
        in_band = (dchunk >= 0) & (dchunk <= A_LEFT_CHUNKS)
        for h in range(HEADS):
            base = jnp.broadcast_to(base_ref[h:h + 1, :], (TQ_A, BASE_A))
            toeplitz = pltpu.roll(base, 0, 1, stride=1, stride_axis=0)
            bias_ref[h] = jnp.where(in_band, toeplitz[:, :WIN_A], NEG_INF)

    @pl.when(qt == 0)
    def _():
        kpad_ref[0:PAD_A, :] = jnp.zeros((PAD_A, GROUP_W), BF16)
        vpad_ref[0:PAD_A, :] = jnp.zeros((PAD_A, GROUP_W), BF16)
        kpad_ref[PAD_A:PAD_A + S, :] = k_ref[0]
        vpad_ref[PAD_A:PAD_A + S, :] = v_ref[0]

    start = pl.multiple_of(qt * TQ_A, TQ_A)
    kw = kpad_ref[pl.ds(start, WIN_A), :]
    vw = vpad_ref[pl.ds(start, WIN_A), :]
    q = q_ref[0]
    col = lax.broadcasted_iota(jnp.int32, (TQ_A, WIN_A), 1)
    real_key = col >= PAD_A - qt * TQ_A
    outs = []
    for h in range(HEADS):
        sl = slice(h * HEAD_DIM, (h + 1) * HEAD_DIM)
        s = _dot_nt(q[:, sl], kw[:, sl]) + bias_ref[h]
        s = jnp.where(real_key, s, NEG_INF)
        m = jnp.max(s, axis=1, keepdims=True)
        p = jnp.exp(s - m)
        l = jnp.sum(p, axis=1, keepdims=True)
        outs.append(_dot(p.astype(BF16), vw[:, sl]) / l)
    o_ref[0] = jnp.concatenate(outs, axis=1).astype(BF16)


def _band_call(q, k, v, base):
    B, S, _ = q.shape
    return pl.pallas_call(
        _band_body,
        name="band",
        grid=(B, S // TQ_A),
        in_specs=[pl.BlockSpec((1, TQ_A, GROUP_W), lambda b, i: (b, i, 0)),
                  pl.BlockSpec((1, S, GROUP_W), lambda b, i: (b, 0, 0)),
                  pl.BlockSpec((1, S, GROUP_W), lambda b, i: (b, 0, 0)),
                  pl.BlockSpec((HEADS, BASE_A), lambda b, i: (0, 0))],
        out_specs=pl.BlockSpec((1, TQ_A, GROUP_W), lambda b, i: (b, i, 0)),
        out_shape=jax.ShapeDtypeStruct((B, S, GROUP_W), BF16),
        scratch_shapes=[pltpu.VMEM((PAD_A + S, GROUP_W), BF16),
                        pltpu.VMEM((PAD_A + S, GROUP_W), BF16),
                        pltpu.VMEM((HEADS, TQ_A, WIN_A), F32)],
        compiler_params=pltpu.CompilerParams(
            dimension_semantics=("arbitrary", "arbitrary"),
            vmem_limit_bytes=VMEM_LIMIT),
    )(q, k, v, base)


def _band_bias_base(rel_bias):
    far = rel_bias[:, 2 * REL_CLIP:]
    n_far = PAD_A - REL_CLIP
    near = rel_bias[:, :0:-1]
    return jnp.concatenate([jnp.broadcast_to(far, (HEADS, n_far)), near,
                            jnp.broadcast_to(far, (HEADS, BASE_A - n_far - 2 * REL_CLIP))],
                           axis=1).astype(F32)


def _online_update_t(st, vt, carry, exp_scale=None):
    m, l, acc = carry
    m_new = jnp.maximum(m, jnp.max(st, axis=0, keepdims=True))
    m_safe = jnp.where(m_new == NEG_INF, 0.0, m_new)
    if exp_scale is None:
        alpha = jnp.exp(m - m_safe)
        p = jnp.exp(st - m_safe)
    else:
        alpha = jnp.exp2((m - m_safe) * exp_scale)
        p = jnp.exp2((st - m_safe) * exp_scale)
    l = alpha * l + jnp.sum(p, axis=0, keepdims=True)
    acc = alpha * acc + _dot(vt, p.astype(BF16))
    return m_new, l, acc


def _online_init_t(tq, d):
    return (jnp.full((1, tq), NEG_INF, F32), jnp.zeros((1, tq), F32), jnp.zeros((d, tq), F32))


def _masked_queries(q, width):
    lane = lax.broadcasted_iota(jnp.int32, (q.shape[0], LANES), 1)
    per_group = LANES // width
    out = []
    for u in range(GROUP_W // width):
        grp = q[:, (u // per_group) * LANES:(u // per_group + 1) * LANES]
        out.append(jnp.where((lane // width) == (u % per_group), grp, 0))
    return out


def _lane_group(x, u, width):
    g = u // (LANES // width)
    return x[:, g * LANES:(g + 1) * LANES]


def _sparse_body(q_ref, k_ref, vt_ref, iq_ref, small_q_ref, small_k_ref, o_ref, key_ref):
    qt = pl.program_id(1)
    tq = tk = TQ_B
    n_kb = qt + 1

    w_t = small_q_ref[0].T[SM_WI:SM_WI + IDX_HEADS, :] * ((IDX_DIM * IDX_HEADS) ** -0.5)
    iq = iq_ref[0]
    iq_heads = [iq[:, g * IDX_DIM:(g + 1) * IDX_DIM] for g in range(IDX_HEADS)]
    krow = lax.broadcasted_iota(jnp.int32, (tk, tq), 0)
    qcol = lax.broadcasted_iota(jnp.int32, (tk, tq), 1)
    diag_ok = (krow // CHUNK) <= (qcol // CHUNK)

    def block_at(kb):
        return pl.ds(pl.multiple_of(kb * tk, tk), tk)

    def score_block(kb, _):
        ki = small_k_ref[0, block_at(kb), :][:, SM_KI:SM_KI + IDX_DIM].astype(BF16)
        sc = jnp.zeros((tk, tq), F32)
        for g in range(IDX_HEADS):
            sc = sc + w_t[g:g + 1, :] * jnp.maximum(_dot_nt(ki, iq_heads[g]), 0.0)
        sc = jnp.where(sc == 0.0, 0.0, sc)
        sc = jnp.where((kb < qt) | diag_ok, sc, NEG_INF)
        bits = pltpu.bitcast(sc, jnp.int32)
        key_ref[block_at(kb), :] = jnp.where(bits < 0, bits ^ 0x7FFFFFFF, bits)
        return 0

    lax.fori_loop(0, n_kb, score_block, 0)

    def count(pred):
        def body(kb, cnt):
            hit = jnp.where(pred(key_ref[block_at(kb), :]), 1.0, 0.0)
            return cnt + jnp.sum(hit, axis=0, keepdims=True)
        return lax.fori_loop(0, n_kb, body, jnp.zeros((1, tq), F32))

    c0 = count(lambda blk: blk >= 0)
    t0 = jnp.where(c0 >= TOP_K, 0, INT_MIN).astype(jnp.int32)
    n_ge0 = jnp.where(c0 >= TOP_K, c0, (n_kb * tk).astype(F32))

    def search(it, carry):
        t, n_ge = carry
        cand = t | jnp.left_shift(jnp.int32(1), 30 - it)
        c = count(lambda blk: blk >= cand)
        ok = c >= TOP_K
        return jnp.where(ok, cand, t), jnp.where(ok, c, n_ge)

    thr, n_ge = lax.fori_loop(0, 31, search, (t0, n_ge0))
    n_gt = count(lambda blk: blk > thr)
    short = thr <= NEG_INF_KEY
    thr = jnp.where(short, NEG_INF_KEY + 1, thr)
    need = jnp.where(short, float(TOP_K), TOP_K - n_gt)
    n_tie = jnp.where(short, 0.0, n_ge - n_gt)
    many_ties = jnp.max(n_tie) > 1.5

    def store_mask(kb, sel):
        key_ref[block_at(kb), :] = pltpu.bitcast(jnp.where(sel, -NEG_INF, NEG_INF), jnp.int32)

    @pl.when(jnp.logical_not(many_ties))
    def _():
        def body(kb, _):
            store_mask(kb, key_ref[block_at(kb), :] >= thr)
            return 0
        lax.fori_loop(0, n_kb, body, 0)

    @pl.when(many_ties)
    def _():
        kr = lax.broadcasted_iota(jnp.int32, (tk, tk), 0)
        kc = lax.broadcasted_iota(jnp.int32, (tk, tk), 1)
        earlier = jnp.where(kc < kr, 1.0, 0.0).astype(BF16)

        def body(kb, ties_seen):
            blk = key_ref[block_at(kb), :]
            tie = blk == thr
            tie_f = jnp.where(tie, 1.0, 0.0)
            rank = _dot(earlier, tie_f.astype(BF16)) + ties_seen
            store_mask(kb, (blk > thr) | (tie & (rank < need)))
            return ties_seen + jnp.sum(tie_f, axis=0, keepdims=True)
        lax.fori_loop(0, n_kb, body, jnp.zeros((1, tq), F32))

    qm = _masked_queries(q_ref[0], HEAD_DIM)

    def attend_block(kb, carry):
        cap = pltpu.bitcast(key_ref[block_at(kb), :], F32)
        kblk = k_ref[0, block_at(kb), :]
        qk = [_dot_nt(_lane_group(kblk, h, HEAD_DIM), qm[h]) for h in range(HEADS)]
        new = []
        for h in range(HEADS):
            st = jnp.minimum(qk[h], cap)
            vt = vt_ref[0, h * HEAD_DIM:(h + 1) * HEAD_DIM, block_at(kb)]
            new.append(_online_update_t(st, vt, carry[h]))
        return tuple(new)

    init = tuple(_online_init_t(tq, HEAD_DIM) for _ in range(HEADS))
    heads = lax.fori_loop(0, n_kb, attend_block, init)
    out_t = jnp.concatenate([acc / l for (_, l, acc) in heads], axis=0)
    o_ref[0] = out_t.T.astype(BF16)


def _sparse_call(q, k, vt, iq, small):
    B, S, _ = q.shape
    qspec = lambda w_: pl.BlockSpec((1, TQ_B, w_), lambda b, i: (b, i, 0))
    full = lambda w_: pl.BlockSpec((1, S, w_), lambda b, i: (b, 0, 0))
    return pl.pallas_call(
        _sparse_body,
        name="sparse",
        grid=(B, S // TQ_B),
        in_specs=[qspec(GROUP_W), full(GROUP_W),
                  pl.BlockSpec((1, GROUP_W, S), lambda b, i: (b, 0, 0)),
                  qspec(GROUP_W), qspec(SMALL_W), full(SMALL_W)],
        out_specs=qspec(GROUP_W),
        out_shape=jax.ShapeDtypeStruct((B, S, GROUP_W), BF16),
        scratch_shapes=[pltpu.VMEM((S, TQ_B), jnp.int32)],
        compiler_params=pltpu.CompilerParams(
            dimension_semantics=("arbitrary", "arbitrary"),
            vmem_limit_bytes=VMEM_LIMIT),
    )(q, k, vt, iq, small, small)


def _forget_body(q_ref, k_ref, vt_ref, cum_ref, o_ref):
    qt = pl.program_id(1)
    tq = tk = TQ_C
    qm = _masked_queries(q_ref[0], HEAD_DIM)
    krow = lax.broadcasted_iota(jnp.int32, (tk, tq), 0)
    qcol = lax.broadcasted_iota(jnp.int32, (tk, tq), 1)
    causal = krow <= qcol

    def block(kb, carry, diag):
        at = pl.ds(pl.multiple_of(kb * tk, tk), tk)
        kblk = k_ref[0, at, :]
        cum = cum_ref[0, at, :]
        qk = [_dot_nt(_lane_group(kblk, h, HEAD_DIM), qm[h]) for h in range(HEADS)]
        new = []
        for h in range(HEADS):
            st = qk[h] - cum[:, SM_FG + h:SM_FG + h + 1]
            if diag:
                st = jnp.where(causal, st, NEG_INF)
            vt = vt_ref[0, h * HEAD_DIM:(h + 1) * HEAD_DIM, at]
            new.append(_online_update_t(st, vt, carry[h]))
        return tuple(new)

    init = tuple(_online_init_t(tq, HEAD_DIM) for _ in range(HEADS))
    carry = lax.fori_loop(0, qt, functools.partial(block, diag=False), init)
    heads = block(qt, carry, True)
    out_t = jnp.concatenate([acc / l for (_, l, acc) in heads], axis=0)
    o_ref[0] = out_t.T.astype(BF16)


def _forget_call(q, k, vt, cum):
    B, S, _ = q.shape
    return pl.pallas_call(
        _forget_body,
        name="forget",
        grid=(B, S // TQ_C),
        in_specs=[pl.BlockSpec((1, TQ_C, GROUP_W), lambda b, i: (b, i, 0)),
                  pl.BlockSpec((1, S, GROUP_W), lambda b, i: (b, 0, 0)),
                  pl.BlockSpec((1, GROUP_W, S), lambda b, i: (b, 0, 0)),
                  pl.BlockSpec((1, S, SMALL_W), lambda b, i: (b, 0, 0))],
        out_specs=pl.BlockSpec((1, TQ_C, GROUP_W), lambda b, i: (b, i, 0)),
        out_shape=jax.ShapeDtypeStruct((B, S, GROUP_W), BF16),
        compiler_params=pltpu.CompilerParams(
            dimension_semantics=("arbitrary", "arbitrary"),
            vmem_limit_bytes=VMEM_LIMIT),
    )(q, k, vt, cum)


def _diff_body(q_ref, k_ref, vt_ref, lq1_ref, lk1_ref, lq2_ref, lk2_ref, g_ref, o_ref, *, lam_init):
    qt = pl.program_id(1)
    tq = tk = TQ_D
    lam = (jnp.exp(jnp.sum(lq1_ref[...] * lk1_ref[...], axis=1, keepdims=True))
           - jnp.exp(jnp.sum(lq2_ref[...] * lk2_ref[...], axis=1, keepdims=True)) + lam_init)
    units = 2 * HEADS
    qm = _masked_queries(q_ref[0], DIFF_DIM)
    krow = lax.broadcasted_iota(jnp.int32, (tk, tq), 0)
    qcol = lax.broadcasted_iota(jnp.int32, (tk, tq), 1)
    chunk_ok = (krow // CHUNK) <= (qcol // CHUNK)
    exp_scale = DIFF_DIM ** -0.5 * LOG2E

    def block(kb, carry, diag):
        at = pl.ds(pl.multiple_of(kb * tk, tk), tk)
        kblk = k_ref[0, at, :]
        qk = [_dot_nt(_lane_group(kblk, u, DIFF_DIM), qm[u]) for u in range(units)]
        new = []
        for u in range(units):
            h = u // 2
            st = qk[u]
            if diag:
                st = jnp.where(chunk_ok, st, NEG_INF)
            vt = vt_ref[0, h * HEAD_DIM:(h + 1) * HEAD_DIM, at]
            new.append(_online_update_t(st, vt, carry[u], exp_scale))
        return tuple(new)

    init = tuple(_online_init_t(tq, HEAD_DIM) for _ in range(units))
    carry = lax.fori_loop(0, qt, functools.partial(block, diag=False), init)
    fin = block(qt, carry, True)
    outs = []
    for h in range(HEADS):
        (_, l1, acc1), (_, l2, acc2) = fin[2 * h], fin[2 * h + 1]
        o = acc1 / l1 - lam * (acc2 / l2)
        o = o * lax.rsqrt(jnp.mean(o * o, axis=0, keepdims=True) + EPS) * g_ref[:, :1]
        outs.append(o * (1.0 - lam_init))
    o_ref[0] = jnp.concatenate(outs, axis=0).T.astype(BF16)


def _diff_call(q, k, vt, lq1, lk1, lq2, lk2, g_col, lam_init):
    B, S, _ = q.shape
    vec = lambda n: pl.BlockSpec((1, n), lambda b, i: (0, 0))
    return pl.pallas_call(
        functools.partial(_diff_body, lam_init=lam_init),
        name="diff",
        grid=(B, S // TQ_D),
        in_specs=[pl.BlockSpec((1, TQ_D, GROUP_W), lambda b, i: (b, i, 0)),
                  pl.BlockSpec((1, S, GROUP_W), lambda b, i: (b, 0, 0)),
                  pl.BlockSpec((1, GROUP_W, S), lambda b, i: (b, 0, 0)),
                  vec(DIFF_DIM), vec(DIFF_DIM), vec(DIFF_DIM), vec(DIFF_DIM),
                  pl.BlockSpec((HEAD_DIM, LANES), lambda b, i: (0, 0))],
        out_specs=pl.BlockSpec((1, TQ_D, GROUP_W), lambda b, i: (b, i, 0)),
        out_shape=jax.ShapeDtypeStruct((B, S, GROUP_W), BF16),
        compiler_params=pltpu.CompilerParams(
            dimension_semantics=("arbitrary", "arbitrary"),
            vmem_limit_bytes=VMEM_LIMIT),
    )(q, k, vt, lq1, lk1, lq2, lk2, g_col)


def _ffn_body(x_ref, ma_ref, mb_ref, mc_ref, md_ref, wo_ref, g2_ref, wg_ref, wu_ref, wd_ref,
              gf_ref, o_ref, *, final):
    mixed = jnp.concatenate([ma_ref[...], mb_ref[...], mc_ref[...], md_ref[...]], axis=1)
    x1 = x_ref[...] + _dot(mixed, wo_ref[...])
    h2 = _rmsnorm(x1, g2_ref[...]).astype(BF16)
    gate = jax.nn.silu(_dot(h2, wg_ref[...]))
    up = _dot(h2, wu_ref[...])
    x2 = x1 + _dot((gate * up).astype(BF16), wd_ref[...])
    if final:
        x2 = _rmsnorm(x2, gf_ref[...])
    o_ref[...] = x2


def _ffn_call(x, mixed, wo, g2, wg, wu, wd, gf, final):
    T = x.shape[0]
    tm = TM_FFN
    tok = lambda w_: pl.BlockSpec((tm, w_), lambda i: (i, 0))
    const = lambda shape: pl.BlockSpec(shape, lambda i: (0, 0), pipeline_mode=pl.Buffered(1))
    return pl.pallas_call(
        functools.partial(_ffn_body, final=final),
        name="ffn",
        grid=(T // tm,),
        in_specs=[tok(D_MODEL)] + [tok(GROUP_W)] * 4 + [
            const((D_MODEL, D_MODEL)), const((1, D_MODEL)), const((D_MODEL, D_FF)),
            const((D_MODEL, D_FF)), const((D_FF, D_MODEL)), const((1, D_MODEL))],
        out_specs=tok(D_MODEL),
        out_shape=jax.ShapeDtypeStruct((T, D_MODEL), F32),
        compiler_params=pltpu.CompilerParams(
            dimension_semantics=("arbitrary",),
            vmem_limit_bytes=VMEM_LIMIT),
    )(x, *mixed, wo, g2, wg, wu, wd, gf)


def _rope_tables(S):
    pos = jnp.arange(S, dtype=F32)[:, None]

    def table(d, reps):
        inv = ROPE_THETA ** (-jnp.arange(0, d, 2, dtype=F32) / d)
        ang = pos * inv[None, :]
        cos, sin = jnp.cos(ang), jnp.sin(ang)
        return (jnp.tile(jnp.concatenate([cos, cos], axis=1), (1, reps)),
                jnp.tile(jnp.concatenate([-sin, sin], axis=1), (1, reps)))

    cos64, sin64 = table(HEAD_DIM, HEADS)
    cos32, sin32 = table(IDX_DIM, GROUP_W // IDX_DIM)
    coss = jnp.ones((S, SMALL_W), F32).at[:, :IDX_DIM].set(cos32[:, :IDX_DIM])
    sins = jnp.zeros((S, SMALL_W), F32).at[:, :IDX_DIM].set(sin32[:, :IDX_DIM])
    return cos64, sin64, cos32, sin32, coss, sins


def _reorder_w_in(w):
    o = np.cumsum([0, 3 * GROUP_W, 3 * GROUP_W, IDX_HEADS * IDX_DIM, IDX_DIM, IDX_HEADS,
                   3 * GROUP_W, HEADS, 3 * GROUP_W])
    pad = jnp.zeros((w.shape[0], SMALL_W - IDX_DIM - IDX_HEADS - HEADS), w.dtype)
    parts = [w[:, o[0]:o[1]], w[:, o[1]:o[2]], w[:, o[2]:o[3]], w[:, o[5]:o[6]], w[:, o[7]:o[8]],
             w[:, o[3]:o[4]], w[:, o[4]:o[5]], w[:, o[6]:o[7]], pad]
    return jnp.concatenate(parts, axis=1).astype(BF16)


def kernel(x, ln1_g, w_in, rel_bias, forget_b, lam_q1, lam_k1, lam_q2, lam_k2, diff_norm_g,
           w_o, ln2_g, w_gate, w_up, w_down, final_g):
    B, S, D = x.shape
    tabs = _rope_tables(S)
    row = lambda a: a.reshape(1, -1).astype(F32)
    for l in range(DEPTH):
        lam_init = 0.8 - 0.6 * math.exp(-0.3 * l)
        fb = jnp.zeros((1, SMALL_W), F32).at[0, SM_FG:SM_FG + HEADS].set(forget_b[l])
        (aq, ak, av, bq, bk, bvt, iq, cq, ck, cvt, dq, dk, dvt, small, cum) = _proj_call(
            x, row(ln1_g[l]), _reorder_w_in(w_in[l]), tabs, fb)
        out_a = _band_call(aq, ak, av, _band_bias_base(rel_bias[l]))
        out_b = _sparse_call(bq, bk, bvt, iq, small)
        out_c = _forget_call(cq, ck, cvt, cum)
        g_col = jnp.broadcast_to(diff_norm_g[l].reshape(-1, 1).astype(F32), (HEAD_DIM, LANES))
        out_d =_diff_call(dq, dk, dvt, row(lam_q1[l]), row(lam_k1[l]), row(lam_q2[l]),
                           row(lam_k2[l]), g_col, lam_init)
        mixed = [o.reshape(B * S, GROUP_W) for o in (out_a, out_b, out_c, out_d)]
        x = _ffn_call(x.reshape(B * S, D), mixed, w_o[l].astype(BF16), row(ln2_g[l]),
                      w_gate[l].astype(BF16), w_up[l].astype(BF16), w_down[l].astype(BF16),
                      row(final_g), final=(l == DEPTH - 1)).reshape(B, S, D)
    return x
```

```python
import functools
import math

import jax
import jax.numpy as jnp
import numpy as np
from jax import lax
from jax.experimental import pallas as pl
from jax.experimental.pallas import tpu as pltpu

D_MODEL = 1024
SEQ = 2048
DEPTH = 2
CHUNK = 64
HEAD_DIM = 64
HEADS = 4
GROUP_W = HEADS * HEAD_DIM
A_LEFT_CHUNKS = 8
REL_CLIP = 128
IDX_HEADS = 8
IDX_DIM = 32
TOP_K = 256
DIFF_DIM = 32
ROPE_THETA = 10000.0
D_FF = 2816
EPS = 1e-6
LANES = 128

OFF_A = 0
OFF_B = 3 * GROUP_W
OFF_IQ = 6 * GROUP_W
OFF_C = 7 * GROUP_W
OFF_D = 10 * GROUP_W
OFF_SMALL = 13 * GROUP_W
SMALL_W = LANES
PROJ_W = OFF_SMALL + SMALL_W
SM_KI = 0
SM_WI = IDX_DIM
SM_FG = IDX_DIM + IDX_HEADS

VMEM_LIMIT = 56 * 1024 * 1024

TM_PROJ = 512
TM_FFN = 512
TQ_A = 2 * CHUNK
WIN_A = (A_LEFT_CHUNKS + 2) * CHUNK
PAD_A = A_LEFT_CHUNKS * CHUNK
BASE_A = WIN_A + TQ_A
TQ_B = 256
TQ_C = 256
TQ_D = 256

F32 = jnp.float32
BF16 = jnp.bfloat16
NEG_INF = float("-inf")
INT16_MIN = -2 ** 15
NEG_INF_KEY = int(np.int32(np.uint32(0xFF800000) ^ np.uint32(0x7FFFFFFF)))
LOG2E = math.log2(math.e)


def _dot(a, b):
    return jnp.dot(a, b, preferred_element_type=F32)


def _dot_nt(a, b):
    return lax.dot_general(a, b, (((1,), (1,)), ((), ())), preferred_element_type=F32)


def _rmsnorm(x, g):
    return x * lax.rsqrt(jnp.mean(x * x, axis=-1, keepdims=True) + EPS) * g


def _rope(x, cos, sin_signed, half):
    n = x.shape[-1]
    lane = lax.broadcasted_iota(jnp.int32, x.shape, 1)
    first = (lane & half) == 0
    swapped = jnp.where(first, pltpu.roll(x, n - half, 1), pltpu.roll(x, half, 1))
    return x * cos + swapped * sin_signed


def _proj_body(x_ref, g_ref, wt_ref, cos64_ref, sin64_ref, cos32_ref, sin32_ref,
               coss_ref, sins_ref, fb_ref,
               aq_ref, ak_ref, avt_ref, bq_ref, bk_ref, bvt_ref, iq_ref,
               cq_ref, ck_ref, cvt_ref, dq_ref, dk_ref, dvt_ref, small_ref, cum_ref,
               carry_ref):
    i = pl.program_id(1)
    tm = x_ref.shape[1]
    h = _rmsnorm(x_ref[0], g_ref[...]).astype(BF16)
    proj = _dot_nt(h, wt_ref[...])

    def grp(off, j):
        return proj[:, off + j * GROUP_W: off + (j + 1) * GROUP_W]

    scale = HEAD_DIM ** -0.5
    aq_ref[0] = (grp(OFF_A, 0) * scale).astype(BF16)
    ak_ref[0] = grp(OFF_A, 1).astype(BF16)
    avt_ref[0] = grp(OFF_A, 2).T.astype(BF16)

    cos64, sin64 = cos64_ref[...], sin64_ref[...]
    bq_ref[0] = (_rope(grp(OFF_B, 0), cos64, sin64, HEAD_DIM // 2) * scale).astype(BF16)
    bk_ref[0] = _rope(grp(OFF_B, 1), cos64, sin64, HEAD_DIM // 2).astype(BF16)
    bvt_ref[0] = grp(OFF_B, 2).T.astype(BF16)

    cos32, sin32 = cos32_ref[...], sin32_ref[...]
    iq_ref[0] = _rope(grp(OFF_IQ, 0), cos32, sin32, IDX_DIM // 2).astype(BF16)

    cq_ref[0] = (grp(OFF_C, 0) * scale).astype(BF16)
    ck_ref[0] = grp(OFF_C, 1).astype(BF16)
    cvt_ref[0] = grp(OFF_C, 2).T.astype(BF16)

    dq_ref[0] = _rope(grp(OFF_D, 0), cos32, sin32, DIFF_DIM // 2).astype(BF16)
    dk_ref[0] = _rope(grp(OFF_D, 1), cos32, sin32, DIFF_DIM // 2).astype(BF16)
    dvt_ref[0] = grp(OFF_D, 2).T.astype(BF16)

    small = proj[:, OFF_SMALL:OFF_SMALL + SMALL_W]
    small_ref[0] = _rope(small, coss_ref[...], sins_ref[...], IDX_DIM // 2)

    lane = lax.broadcasted_iota(jnp.int32, small.shape, 1)
    is_fg = (lane >= SM_FG) & (lane < SM_FG + HEADS)
    logf = jnp.where(is_fg, jax.nn.log_sigmoid(small + fb_ref[...]), 0.0)
    row = lax.broadcasted_iota(jnp.int32, (tm, tm), 0)
    col = lax.broadcasted_iota(jnp.int32, (tm, tm), 1)
    tri = jnp.where(row >= col, 1.0, 0.0).astype(F32)

    @pl.when(i == 0)
    def _():
        carry_ref[...] = jnp.zeros_like(carry_ref)

    cum = jnp.dot(tri, logf, precision=lax.Precision.HIGHEST,
                  preferred_element_type=F32) + carry_ref[...]
    carry_ref[...] = cum[tm - 1:tm, :]
    cum_ref[0] = cum


def _proj_call(x, g, w, tabs, fb):
    B, S, _ = x.shape
    tm = TM_PROJ
    nt = S // tm
    cos64, sin64, cos32, sin32, coss, sins = tabs
    tok = lambda w_: pl.BlockSpec((1, tm, w_), lambda b, i: (b, i, 0))
    tok_t = pl.BlockSpec((1, GROUP_W, tm), lambda b, i: (b, 0, i))
    tab = lambda w_: pl.BlockSpec((tm, w_), lambda b, i: (i, 0))
    const = lambda shape: pl.BlockSpec(shape, lambda b, i: (0,) * len(shape),
                                       pipeline_mode=pl.Buffered(1))
    bf = jax.ShapeDtypeStruct((B, S, GROUP_W), BF16)
    bf_t = jax.ShapeDtypeStruct((B, GROUP_W, S), BF16)
    small = jax.ShapeDtypeStruct((B, S, SMALL_W), F32)
    out_shape = [bf, bf, bf_t, bf, bf, bf_t, bf, bf, bf, bf_t, bf, bf, bf_t, small, small]
    g_, t_ = tok(GROUP_W), tok_t
    out_specs = [g_, g_, t_, g_, g_, t_, g_, g_, g_, t_, g_, g_, t_, tok(SMALL_W), tok(SMALL_W)]
    return pl.pallas_call(
        _proj_body,
        name="proj",
        grid=(B, nt),
        in_specs=[tok(D_MODEL), const((1, D_MODEL)), const((PROJ_W, D_MODEL)),
                  tab(GROUP_W), tab(GROUP_W), tab(GROUP_W), tab(GROUP_W),
                  tab(SMALL_W), tab(SMALL_W), const((1, SMALL_W))],
        out_specs=out_specs,
        out_shape=out_shape,
        scratch_shapes=[pltpu.VMEM((1, SMALL_W), F32)],
        compiler_params=pltpu.CompilerParams(
            dimension_semantics=("arbitrary", "arbitrary"),
            vmem_limit_bytes=VMEM_LIMIT),
    )(x, g, w, cos64, sin64, cos32, sin32, coss, sins, fb)


def _band_body(q_ref, k_ref, vt_ref, base_ref, o_ref, kpad_ref, vtpad_ref, bias_ref):
    b = pl.program_id(0)
    qt = pl.program_id(1)
    S = k_ref.shape[1]

    @pl.when((b == 0) & (qt == 0))
    def _():
        r = lax.broadcasted_iota(jnp.int32, (TQ_A, WIN_A), 0)
        j = lax.broadcasted_iota(jnp.int32, (TQ_A, WIN_A), 1)
        dchunk = r // CHUNK + A_LEFT_CHUNKS - j // CHUNK
        in_band = (dchunk >= 0) & (dchunk <= A_LEFT_CHUNKS)
        for h in range(HEADS):
            base = jnp.broadcast_to(base_ref[h:h + 1, :], (TQ_A, BASE_A))
            toeplitz = pltpu.roll(base, 0, 1, stride=1, stride_axis=0)
            bias_ref[h] = jnp.where(in_band, toeplitz[:, :WIN_A], NEG_INF).T

    @pl.when(qt == 0)
    def _():
        kpad_ref[0:PAD_A, :] = jnp.zeros((PAD_A, GROUP_W), BF16)
        vtpad_ref[:, 0:PAD_A] = jnp.zeros((GROUP_W, PAD_A), BF16)
        kpad_ref[PAD_A:PAD_A + S, :] = k_ref[0]
        vtpad_ref[:, PAD_A:PAD_A + S] = vt_ref[0]

    start = pl.multiple_of(qt * TQ_A, TQ_A)
    kw = kpad_ref[pl.ds(start, WIN_A), :]
    qm = _masked_queries(q_ref[0], HEAD_DIM)
    krow = lax.broadcasted_iota(jnp.int32, (WIN_A, TQ_A), 0)
    real_key = krow >= PAD_A - qt * TQ_A
    qk = [_dot_nt(_lane_group(kw, h, HEAD_DIM), qm[h]) for h in range(HEADS)]
    outs = []
    for h in range(HEADS):
        st = jnp.where(real_key, qk[h] + bias_ref[h], NEG_INF)
        m = jnp.max(st, axis=0, keepdims=True)
        p = jnp.exp(st - m)
        l = jnp.sum(p, axis=0, keepdims=True)
        vt = vtpad_ref[h * HEAD_DIM:(h + 1) * HEAD_DIM, pl.ds(start, WIN_A)]
        outs.append(_dot(vt, p.astype(BF16)) / l)
    o_ref[0] = jnp.concatenate(outs, axis=0).T.astype(BF16)


def _band_call(q, k, vt, base):
    B, S, _ = q.shape
    return pl.pallas_call(
        _band_body,
        name="band",
        grid=(B, S // TQ_A),
        in_specs=[pl.BlockSpec((1, TQ_A, GROUP_W), lambda b, i: (b, i, 0)),
                  pl.BlockSpec((1, S, GROUP_W), lambda b, i: (b, 0, 0)),
                  pl.BlockSpec((1, GROUP_W, S), lambda b, i: (b, 0, 0)),
                  pl.BlockSpec((HEADS, BASE_A), lambda b, i: (0, 0))],
        out_specs=pl.BlockSpec((1, TQ_A, GROUP_W), lambda b, i: (b, i, 0)),
        out_shape=jax.ShapeDtypeStruct((B, S, GROUP_W), BF16),
        scratch_shapes=[pltpu.VMEM((PAD_A + S, GROUP_W), BF16),
                        pltpu.VMEM((GROUP_W, PAD_A + S), BF16),
                        pltpu.VMEM((HEADS, WIN_A, TQ_A), F32)],
        compiler_params=pltpu.CompilerParams(
            dimension_semantics=("arbitrary", "arbitrary"),
            vmem_limit_bytes=VMEM_LIMIT),
    )(q, k, vt, base)


def _band_bias_base(rel_bias):
    far = rel_bias[:, 2 * REL_CLIP:]
    n_far = PAD_A - REL_CLIP
    near = rel_bias[:, :0:-1]
    return jnp.concatenate([jnp.broadcast_to(far, (HEADS, n_far)), near,
                            jnp.broadcast_to(far, (HEADS, BASE_A - n_far - 2 * REL_CLIP))],
                           axis=1).astype(F32)


def _online_update_t(st, vt, carry, exp_scale=None):
    m, l, acc = carry
    m_new = jnp.maximum(m, jnp.max(st, axis=0, keepdims=True))
    m_safe = jnp.where(m_new == NEG_INF, 0.0, m_new)
    if exp_scale is None:
        alpha = jnp.exp(m - m_safe)
        p = jnp.exp(st - m_safe)
    else:
        alpha = jnp.exp2((m - m_safe) * exp_scale)
        p = jnp.exp2((st - m_safe) * exp_scale)
    l = alpha * l + jnp.sum(p, axis=0, keepdims=True)
    acc = alpha * acc + _dot(vt, p.astype(BF16))
    return m_new, l, acc


def _online_init_t(tq, d):
    return (jnp.full((1, tq), NEG_INF, F32), jnp.zeros((1, tq), F32), jnp.zeros((d, tq), F32))


def _masked_queries(q, width):
    lane = lax.broadcasted_iota(jnp.int32, (q.shape[0], LANES), 1)
    per_group = LANES // width
    out = []
    for u in range(GROUP_W // width):
        grp = q[:, (u // per_group) * LANES:(u // per_group + 1) * LANES]
        out.append(jnp.where((lane // width) == (u % per_group), grp, 0))
    return out


def _lane_group(x, u, width):
    g = u // (LANES // width)
    return x[:, g * LANES:(g + 1) * LANES]


def _sparse_body(q_ref, k_ref, vt_ref, iq_ref, small_q_ref, small_k_ref, o_ref, key_ref, half_ref):
    qt = pl.program_id(1)
    tq = tk = TQ_B
    n_kb = qt + 1

    w_t = small_q_ref[0].T[SM_WI:SM_WI + IDX_HEADS, :] * ((IDX_DIM * IDX_HEADS) ** -0.5)
    iq = iq_ref[0]
    iq_heads = [iq[:, g * IDX_DIM:(g + 1) * IDX_DIM] for g in range(IDX_HEADS)]
    krow = lax.broadcasted_iota(jnp.int32, (tk, tq), 0)
    qcol = lax.broadcasted_iota(jnp.int32, (tk, tq), 1)
    diag_ok = (krow // CHUNK) <= (qcol // CHUNK)

    def block_at(kb):
        return pl.ds(pl.multiple_of(kb * tk, tk), tk)

    def score_block(kb, _):
        ki = small_k_ref[0, block_at(kb), :][:, SM_KI:SM_KI + IDX_DIM].astype(BF16)
        sc = jnp.zeros((tk, tq), F32)
        for g in range(IDX_HEADS):
            sc = sc + w_t[g:g + 1, :] * jnp.maximum(_dot_nt(ki, iq_heads[g]), 0.0)
        sc = jnp.where(sc == 0.0, 0.0, sc)
        sc = jnp.where((kb < qt) | diag_ok, sc, NEG_INF)
        bits = pltpu.bitcast(sc, jnp.int32)
        key_ref[block_at(kb), :] = jnp.where(bits < 0, bits ^ 0x7FFFFFFF, bits)
        return 0

    lax.fori_loop(0, n_kb, score_block, 0)

    def count(pred):
        def body(kb, cnt):
            hit = jnp.where(pred(key_ref[block_at(kb), :]), 1.0, 0.0)
            return cnt + jnp.sum(hit, axis=0, keepdims=True)
        return lax.fori_loop(0, n_kb, body, jnp.zeros((1, tq), F32))

    def count_half(pred):
        def body(kb, acc):
            return acc + jnp.where(pred(half_ref[block_at(kb), :]), jnp.int16(1), jnp.int16(0))
        acc = lax.fori_loop(0, n_kb, body, jnp.zeros((tk, tq), jnp.int16))
        while acc.shape[0] > 16:
            acc = acc[:acc.shape[0] // 2] + acc[acc.shape[0] // 2:]
        return jnp.sum(acc.astype(F32), axis=0, keepdims=True)

    def search_half(above):
        def accept(cand, t):
            c = count_half(lambda blk: blk >= cand.astype(jnp.int16))
            return jnp.where(above + c >= TOP_K, cand, t)
        t0 = accept(jnp.zeros((1, tq), jnp.int32), jnp.full((1, tq), INT16_MIN, jnp.int32))
        return lax.fori_loop(
            0, 15, lambda it, t: accept(t | jnp.left_shift(jnp.int32(1), 14 - it), t), t0)

    def fill_high(kb, _):
        half_ref[block_at(kb), :] = (key_ref[block_at(kb), :] >> 16).astype(jnp.int16)
        return 0

    lax.fori_loop(0, n_kb, fill_high, 0)
    thr_hi = search_half(0.0)
    n_above = count_half(lambda blk: blk > thr_hi.astype(jnp.int16))

    def fill_low(kb, _):
        key = key_ref[block_at(kb), :]
        low = (key & 0xFFFF) - 0x8000
        half_ref[block_at(kb), :] = jnp.where((key >> 16) == thr_hi, low, INT16_MIN).astype(jnp.int16)
        return 0

    lax.fori_loop(0, n_kb, fill_low, 0)
    thr_lo = search_half(n_above)
    thr = (thr_hi << 16) | (thr_lo + 0x8000)
    n_ge = count(lambda blk: blk >= thr)
    n_gt = count(lambda blk: blk > thr)
    short = thr <= NEG_INF_KEY
    thr = jnp.where(short, NEG_INF_KEY + 1, thr)
    need = jnp.where(short, float(TOP_K), TOP_K - n_gt)
    n_tie = jnp.where(short, 0.0, n_ge - n_gt)
    many_ties = jnp.max(n_tie) > 1.5

    def store_mask(kb, sel):
        key_ref[block_at(kb), :] = pltpu.bitcast(jnp.where(sel, -NEG_INF, NEG_INF), jnp.int32)

    @pl.when(jnp.logical_not(many_ties))
    def _():
        def body(kb, _):
            store_mask(kb, key_ref[block_at(kb), :] >= thr)
            return 0
        lax.fori_loop(0, n_kb, body, 0)

    @pl.when(many_ties)
    def _():
        kr = lax.broadcasted_iota(jnp.int32, (tk, tk), 0)
        kc = lax.broadcasted_iota(jnp.int32, (tk, tk), 1)
        earlier = jnp.where(kc < kr, 1.0, 0.0).astype(BF16)

        def body(kb, ties_seen):
            blk = key_ref[block_at(kb), :]
            tie = blk == thr
            tie_f = jnp.where(tie, 1.0, 0.0)
            rank = _dot(earlier, tie_f.astype(BF16)) + ties_seen
            store_mask(kb, (blk > thr) | (tie & (rank < need)))
            return ties_seen + jnp.sum(tie_f, axis=0, keepdims=True)
        lax.fori_loop(0, n_kb, body, jnp.zeros((1, tq), F32))

    qm = _masked_queries(q_ref[0], HEAD_DIM)

    def attend_block(kb, carry):
        cap = pltpu.bitcast(key_ref[block_at(kb), :], F32)
        kblk = k_ref[0, block_at(kb), :]
        qk = [_dot_nt(_lane_group(kblk, h, HEAD_DIM), qm[h]) for h in range(HEADS)]
        new = []
        for h in range(HEADS):
            st = jnp.minimum(qk[h], cap)
            vt = vt_ref[0, h * HEAD_DIM:(h + 1) * HEAD_DIM, block_at(kb)]
            new.append(_online_update_t(st, vt, carry[h]))
        return tuple(new)

    init = tuple(_online_init_t(tq, HEAD_DIM) for _ in range(HEADS))
    heads = lax.fori_loop(0, n_kb, attend_block, init)
    out_t = jnp.concatenate([acc / l for (_, l, acc) in heads], axis=0)
    o_ref[0] = out_t.T.astype(BF16)


def _sparse_call(q, k, vt, iq, small):
    B, S, _ = q.shape
    qspec = lambda w_: pl.BlockSpec((1, TQ_B, w_), lambda b, i: (b, i, 0))
    full = lambda w_: pl.BlockSpec((1, S, w_), lambda b, i: (b, 0, 0))
    return pl.pallas_call(
        _sparse_body,
        name="sparse",
        grid=(B, S // TQ_B),
        in_specs=[qspec(GROUP_W), full(GROUP_W),
                  pl.BlockSpec((1, GROUP_W, S), lambda b, i: (b, 0, 0)),
                  qspec(GROUP_W), qspec(SMALL_W), full(SMALL_W)],
        out_specs=qspec(GROUP_W),
        out_shape=jax.ShapeDtypeStruct((B, S, GROUP_W), BF16),
        scratch_shapes=[pltpu.VMEM((S, TQ_B), jnp.int32), pltpu.VMEM((S, TQ_B), jnp.int16)],
        compiler_params=pltpu.CompilerParams(
            dimension_semantics=("arbitrary", "arbitrary"),
            vmem_limit_bytes=VMEM_LIMIT),
    )(q, k, vt, iq, small, small)


def _forget_body(q_ref, k_ref, vt_ref, cum_ref, o_ref):
    qt = pl.program_id(1)
    tq = tk = TQ_C
    qm = _masked_queries(q_ref[0], HEAD_DIM)
    krow = lax.broadcasted_iota(jnp.int32, (tk, tq), 0)
    qcol = lax.broadcasted_iota(jnp.int32, (tk, tq), 1)
    causal = krow <= qcol

    def block(kb, carry, diag):
        at = pl.ds(pl.multiple_of(kb * tk, tk), tk)
        kblk = k_ref[0, at, :]
        cum = cum_ref[0, at, :]
        qk = [_dot_nt(_lane_group(kblk, h, HEAD_DIM), qm[h]) for h in range(HEADS)]
        new = []
        for h in range(HEADS):
            st = qk[h] - cum[:, SM_FG + h:SM_FG + h + 1]
            if diag:
                st = jnp.where(causal, st, NEG_INF)
            vt = vt_ref[0, h * HEAD_DIM:(h + 1) * HEAD_DIM, at]
            new.append(_online_update_t(st, vt, carry[h]))
        return tuple(new)

    init = tuple(_online_init_t(tq, HEAD_DIM) for _ in range(HEADS))
    carry = lax.fori_loop(0, qt, functools.partial(block, diag=False), init)
    heads = block(qt, carry, True)
    out_t = jnp.concatenate([acc / l for (_, l, acc) in heads], axis=0)
    o_ref[0] = out_t.T.astype(BF16)


def _forget_call(q, k, vt, cum):
    B, S, _ = q.shape
    return pl.pallas_call(
        _forget_body,
        name="forget",
        grid=(B, S // TQ_C),
        in_specs=[pl.BlockSpec((1, TQ_C, GROUP_W), lambda b, i: (b, i, 0)),
                  pl.BlockSpec((1, S, GROUP_W), lambda b, i: (b, 0, 0)),
                  pl.BlockSpec((1, GROUP_W, S), lambda b, i: (b, 0, 0)),
                  pl.BlockSpec((1, S, SMALL_W), lambda b, i: (b, 0, 0))],
        out_specs=pl.BlockSpec((1, TQ_C, GROUP_W), lambda b, i: (b, i, 0)),
        out_shape=jax.ShapeDtypeStruct((B, S, GROUP_W), BF16),
        compiler_params=pltpu.CompilerParams(
            dimension_semantics=("arbitrary", "arbitrary"),
            vmem_limit_bytes=VMEM_LIMIT),
    )(q, k, vt, cum)


def _diff_body(q_ref, k_ref, vt_ref, lq1_ref, lk1_ref, lq2_ref, lk2_ref, g_ref, o_ref, *, lam_init):
    qt = pl.program_id(1)
    tq = tk = TQ_D
    lam = (jnp.exp(jnp.sum(lq1_ref[...] * lk1_ref[...], axis=1, keepdims=True))
           - jnp.exp(jnp.sum(lq2_ref[...] * lk2_ref[...], axis=1, keepdims=True)) + lam_init)
    units = 2 * HEADS
    qm = _masked_queries(q_ref[0], DIFF_DIM)
    krow = lax.broadcasted_iota(jnp.int32, (tk, tq), 0)
    qcol = lax.broadcasted_iota(jnp.int32, (tk, tq), 1)
    chunk_ok = (krow // CHUNK) <= (qcol // CHUNK)
    exp_scale = DIFF_DIM ** -0.5 * LOG2E

    def block(kb, carry, diag):
        at = pl.ds(pl.multiple_of(kb * tk, tk), tk)
        kblk = k_ref[0, at, :]
        qk = [_dot_nt(_lane_group(kblk, u, DIFF_DIM), qm[u]) for u in range(units)]
        new = []
        for u in range(units):
            h = u // 2
            st = qk[u]
            if diag:
                st = jnp.where(chunk_ok, st, NEG_INF)
            vt = vt_ref[0, h * HEAD_DIM:(h + 1) * HEAD_DIM, at]
            new.append(_online_update_t(st, vt, carry[u], exp_scale))
        return tuple(new)

    init = tuple(_online_init_t(tq, HEAD_DIM) for _ in range(units))
    carry = lax.fori_loop(0, qt, functools.partial(block, diag=False), init)
    fin = block(qt, carry, True)
    outs = []
    for h in range(HEADS):
        (_, l1, acc1), (_, l2, acc2) = fin[2 * h], fin[2 * h + 1]
        o = acc1 / l1 - lam * (acc2 / l2)
        o = o * lax.rsqrt(jnp.mean(o * o, axis=0, keepdims=True) + EPS) * g_ref[:, :1]
        outs.append(o * (1.0 - lam_init))
    o_ref[0] = jnp.concatenate(outs, axis=0).T.astype(BF16)


def _diff_call(q, k, vt, lq1, lk1, lq2, lk2, g_col, lam_init):
    B, S, _ = q.shape
    vec = lambda n: pl.BlockSpec((1, n), lambda b, i: (0, 0))
    return pl.pallas_call(
        functools.partial(_diff_body, lam_init=lam_init),
        name="diff",
        grid=(B, S // TQ_D),
        in_specs=[pl.BlockSpec((1, TQ_D, GROUP_W), lambda b, i: (b, i, 0)),
                  pl.BlockSpec((1, S, GROUP_W), lambda b, i: (b, 0, 0)),
                  pl.BlockSpec((1, GROUP_W, S), lambda b, i: (b, 0, 0)),
                  vec(DIFF_DIM), vec(DIFF_DIM), vec(DIFF_DIM), vec(DIFF_DIM),
                  pl.BlockSpec((HEAD_DIM, LANES), lambda b, i: (0, 0))],
        out_specs=pl.BlockSpec((1, TQ_D, GROUP_W), lambda b, i: (b, i, 0)),
        out_shape=jax.ShapeDtypeStruct((B, S, GROUP_W), BF16),
        compiler_params=pltpu.CompilerParams(
            dimension_semantics=("arbitrary", "arbitrary"),
            vmem_limit_bytes=VMEM_LIMIT),
    )(q, k, vt, lq1, lk1, lq2, lk2, g_col)


def _ffn_body(x_ref, ma_ref, mb_ref, mc_ref, md_ref, wo_ref, g2_ref, wg_ref, wu_ref, wd_ref,
              gf_ref, o_ref, *, final):
    mixed = jnp.concatenate([ma_ref[...], mb_ref[...], mc_ref[...], md_ref[...]], axis=1)
    x1 = x_ref[...] + _dot(mixed, wo_ref[...])
    h2 = _rmsnorm(x1, g2_ref[...]).astype(BF16)
    gate = jax.nn.silu(_dot(h2, wg_ref[...]))
    up = _dot(h2, wu_ref[...])
    x2 = x1 + _dot((gate * up).astype(BF16), wd_ref[...])
    if final:
        x2 = _rmsnorm(x2, gf_ref[...])
    o_ref[...] = x2


def _ffn_call(x, mixed, wo, g2, wg, wu, wd, gf, final):
    T = x.shape[0]
    tm = TM_FFN
    tok = lambda w_: pl.BlockSpec((tm, w_), lambda i: (i, 0))
    const = lambda shape: pl.BlockSpec(shape, lambda i: (0, 0), pipeline_mode=pl.Buffered(1))
    return pl.pallas_call(
        functools.partial(_ffn_body, final=final),
        name="ffn",
        grid=(T // tm,),
        in_specs=[tok(D_MODEL)] + [tok(GROUP_W)] * 4 + [
            const((D_MODEL, D_MODEL)), const((1, D_MODEL)), const((D_MODEL, D_FF)),
            const((D_MODEL, D_FF)), const((D_FF, D_MODEL)), const((1, D_MODEL))],
        out_specs=tok(D_MODEL),
        out_shape=jax.ShapeDtypeStruct((T, D_MODEL), F32),
        compiler_params=pltpu.CompilerParams(
            dimension_semantics=("arbitrary",),
            vmem_limit_bytes=VMEM_LIMIT),
    )(x, *mixed, wo, g2, wg, wu, wd, gf)


def _rope_tables(S):
    pos = jnp.arange(S, dtype=F32)[:, None]

    def table(d, reps):
        inv = ROPE_THETA ** (-jnp.arange(0, d, 2, dtype=F32) / d)
        ang = pos * inv[None, :]
        cos, sin = jnp.cos(ang), jnp.sin(ang)
        return (jnp.tile(jnp.concatenate([cos, cos], axis=1), (1, reps)),
                jnp.tile(jnp.concatenate([-sin, sin], axis=1), (1, reps)))

    cos64, sin64 = table(HEAD_DIM, HEADS)
    cos32, sin32 = table(IDX_DIM, GROUP_W // IDX_DIM)
    coss = jnp.ones((S, SMALL_W), F32).at[:, :IDX_DIM].set(cos32[:, :IDX_DIM])
    sins = jnp.zeros((S, SMALL_W), F32).at[:, :IDX_DIM].set(sin32[:, :IDX_DIM])
    return cos64, sin64, cos32, sin32, coss, sins


def _reorder_w_in(w):
    o = np.cumsum([0, 3 * GROUP_W, 3 * GROUP_W, IDX_HEADS * IDX_DIM, IDX_DIM, IDX_HEADS,
                   3 * GROUP_W, HEADS, 3 * GROUP_W])
    wt = w.T
    pad = jnp.zeros((SMALL_W - IDX_DIM - IDX_HEADS - HEADS, w.shape[0]), w.dtype)
    parts = [wt[o[0]:o[1]], wt[o[1]:o[2]], wt[o[2]:o[3]], wt[o[5]:o[6]], wt[o[7]:o[8]],
             wt[o[3]:o[4]], wt[o[4]:o[5]], wt[o[6]:o[7]], pad]
    return jnp.concatenate(parts, axis=0).astype(BF16)


def kernel(x, ln1_g, w_in, rel_bias, forget_b, lam_q1, lam_k1, lam_q2, lam_k2, diff_norm_g,
           w_o, ln2_g, w_gate, w_up, w_down, final_g):
    B, S, D = x.shape
    tabs = _rope_tables(S)
    row = lambda a: a.reshape(1, -1).astype(F32)
    for l in range(DEPTH):
        lam_init = 0.8 - 0.6 * math.exp(-0.3 * l)
        fb = jnp.zeros((1, SMALL_W), F32).at[0, SM_FG:SM_FG + HEADS].set(forget_b[l])
        (aq, ak, avt, bq, bk, bvt, iq, cq, ck, cvt, dq, dk, dvt, small, cum) = _proj_call(
            x, row(ln1_g[l]), _reorder_w_in(w_in[l]), tabs, fb)
        out_a = _band_call(aq, ak, avt, _band_bias_base(rel_bias[l]))
        out_b = _sparse_call(bq, bk, bvt, iq, small)
        out_c = _forget_call(cq, ck, cvt, cum)
        g_col = jnp.broadcast_to(diff_norm_g[l].reshape(-1, 1).astype(F32), (HEAD_DIM, LANES))
        out_d =_diff_call(dq, dk, dvt, row(lam_q1[l]), row(lam_k1[l]), row(lam_q2[l]),
                           row(lam_k2[l]), g_col, lam_init)
        mixed = [o.reshape(B * S, GROUP_W) for o in (out_a, out_b, out_c, out_d)]
        x = _ffn_call(x.reshape(B * S, D), mixed, w_o[l].astype(BF16), row(ln2_g[l]),
                      w_gate[l].astype(BF16), w_up[l].astype(BF16), w_down[l].astype(BF16),
                      row(final_g), final=(l == DEPTH - 1)).reshape(B, S, D)
    return x
```

```python
import functools
import math

import jax
import jax.numpy as jnp
import numpy as np
from jax import lax
from jax.experimental import pallas as pl
from jax.experimental.pallas import tpu as pltpu

D_MODEL = 1024
SEQ = 2048
DEPTH = 2
CHUNK = 64
HEAD_DIM = 64
HEADS = 4
GROUP_W = HEADS * HEAD_DIM
A_LEFT_CHUNKS = 8
REL_CLIP = 128
IDX_HEADS = 8
IDX_DIM = 32
TOP_K = 256
DIFF_DIM = 32
ROPE_THETA = 10000.0
D_FF = 2816
EPS = 1e-6
LANES = 128

OFF_A = 0
OFF_B = 3 * GROUP_W
OFF_IQ = 6 * GROUP_W
OFF_C = 7 * GROUP_W
OFF_D = 10 * GROUP_W
OFF_SMALL = 13 * GROUP_W
SMALL_W = LANES
PROJ_W = OFF_SMALL + SMALL_W
SM_KI = 0
SM_WI = IDX_DIM
SM_FG = IDX_DIM + IDX_HEADS

VMEM_LIMIT = 56 * 1024 * 1024

TM_PROJ = 512
CUM_SUB = 128
TM_FFN = 512
TQ_A = 2 * CHUNK
SUB_A = 2
WIN_A = (A_LEFT_CHUNKS + 2) * CHUNK
PAD_A = A_LEFT_CHUNKS * CHUNK
BASE_A = WIN_A + TQ_A
TQ_B = 256
TQ_C = 512
TK_C = 256
TQ_D = 512
TK_D = 256

F32 = jnp.float32
BF16 = jnp.bfloat16
NEG_INF = float("-inf")
INT16_MIN = -2 ** 15
NEG_INF_KEY = int(np.int32(np.uint32(0xFF800000) ^ np.uint32(0x7FFFFFFF)))
LOG2E = math.log2(math.e)


def _dot(a, b):
    return jnp.dot(a, b, preferred_element_type=F32)


def _dot_nt(a, b):
    return lax.dot_general(a, b, (((1,), (1,)), ((), ())), preferred_element_type=F32)


def _rmsnorm(x, g):
    return x * lax.rsqrt(jnp.mean(x * x, axis=-1, keepdims=True) + EPS) * g


def _rope(x, cos, sin_signed, half):
    n = x.shape[-1]
    lane = lax.broadcasted_iota(jnp.int32, x.shape, 1)
    first = (lane & half) == 0
    swapped = jnp.where(first, pltpu.roll(x, n - half, 1), pltpu.roll(x, half, 1))
    return x * cos + swapped * sin_signed


def _proj_body(x_ref, g_ref, wt_ref, cos64_ref, sin64_ref, cos32_ref, sin32_ref,
               coss_ref, sins_ref, fb_ref,
               aq_ref, ak_ref, avt_ref, bq_ref, bk_ref, bvt_ref, iq_ref,
               cq_ref, ck_ref, cvt_ref, dq_ref, dk_ref, dvt_ref, small_ref, cum_ref,
               carry_ref):
    i = pl.program_id(1)
    tm = x_ref.shape[1]
    h = _rmsnorm(x_ref[0], g_ref[...]).astype(BF16)

    def grp(off, j):
        return _dot_nt(h, wt_ref[off + j * GROUP_W: off + (j + 1) * GROUP_W, :])

    scale = HEAD_DIM ** -0.5
    aq_ref[0] = (grp(OFF_A, 0) * scale).astype(BF16)
    ak_ref[0] = grp(OFF_A, 1).astype(BF16)
    avt_ref[0] = grp(OFF_A, 2).T.astype(BF16)

    cos64, sin64 = cos64_ref[...], sin64_ref[...]
    bq_ref[0] = (_rope(grp(OFF_B, 0), cos64, sin64, HEAD_DIM // 2) * scale).astype(BF16)
    bk_ref[0] = _rope(grp(OFF_B, 1), cos64, sin64, HEAD_DIM // 2).astype(BF16)
    bvt_ref[0] = grp(OFF_B, 2).T.astype(BF16)

    cos32, sin32 = cos32_ref[...], sin32_ref[...]
    iq_ref[0] = _rope(grp(OFF_IQ, 0), cos32, sin32, IDX_DIM // 2).astype(BF16)

    cq_ref[0] = (grp(OFF_C, 0) * scale).astype(BF16)
    ck_ref[0] = grp(OFF_C, 1).astype(BF16)
    cvt_ref[0] = grp(OFF_C, 2).T.astype(BF16)

    dq_ref[0] = _rope(grp(OFF_D, 0), cos32, sin32, DIFF_DIM // 2).astype(BF16)
    dk_ref[0] = _rope(grp(OFF_D, 1), cos32, sin32, DIFF_DIM // 2).astype(BF16)
    dvt_ref[0] = grp(OFF_D, 2).T.astype(BF16)

    small = _dot_nt(h, wt_ref[OFF_SMALL:OFF_SMALL + SMALL_W, :])
    small_ref[0] = _rope(small, coss_ref[...], sins_ref[...], IDX_DIM // 2)

    lane = lax.broadcasted_iota(jnp.int32, small.shape, 1)
    is_fg = (lane >= SM_FG) & (lane < SM_FG + HEADS)
    logf = jnp.where(is_fg, jax.nn.log_sigmoid(small + fb_ref[...]), 0.0)

    @pl.when(i == 0)
    def _():
        carry_ref[...] = jnp.zeros_like(carry_ref)

    hi = logf.astype(BF16)
    rest = logf - hi.astype(F32)
    mid = rest.astype(BF16)
    lo = (rest - mid.astype(F32)).astype(BF16)
    row = lax.broadcasted_iota(jnp.int32, (CUM_SUB, CUM_SUB), 0)
    col = lax.broadcasted_iota(jnp.int32, (CUM_SUB, CUM_SUB), 1)
    tri = jnp.where(row >= col, 1.0, 0.0).astype(BF16)
    run = carry_ref[...]
    for j in range(tm // CUM_SUB):
        rows = slice(j * CUM_SUB, (j + 1) * CUM_SUB)
        local = _dot(tri, hi[rows]) + _dot(tri, mid[rows]) + _dot(tri, lo[rows])
        cum_ref[0, rows, :] = local + run
        run = run + local[CUM_SUB - 1:CUM_SUB, :]
    carry_ref[...] = run


def _proj_call(x, g, w, tabs, fb):
    B, S, _ = x.shape
    tm = TM_PROJ
    nt = S // tm
    cos64, sin64, cos32, sin32, coss, sins = tabs
    tok = lambda w_: pl.BlockSpec((1, tm, w_), lambda b, i: (b, i, 0))
    tok_t = pl.BlockSpec((1, GROUP_W, tm), lambda b, i: (b, 0, i))
    tab = lambda w_: pl.BlockSpec((tm, w_), lambda b, i: (i, 0))
    const = lambda shape: pl.BlockSpec(shape, lambda b, i: (0,) * len(shape),
                                       pipeline_mode=pl.Buffered(1))
    bf = jax.ShapeDtypeStruct((B, S, GROUP_W), BF16)
    bf_t = jax.ShapeDtypeStruct((B, GROUP_W, S), BF16)
    small = jax.ShapeDtypeStruct((B, S, SMALL_W), F32)
    out_shape = [bf, bf, bf_t, bf, bf, bf_t, bf, bf, bf, bf_t, bf, bf, bf_t, small, small]
    g_, t_ = tok(GROUP_W), tok_t
    out_specs = [g_, g_, t_, g_, g_, t_, g_, g_, g_, t_, g_, g_, t_, tok(SMALL_W), tok(SMALL_W)]
    return pl.pallas_call(
        _proj_body,
        name="proj",
        grid=(B, nt),
        in_specs=[tok(D_MODEL), const((1, D_MODEL)), const((PROJ_W, D_MODEL)),
                  tab(GROUP_W), tab(GROUP_W), tab(GROUP_W), tab(GROUP_W),
                  tab(SMALL_W), tab(SMALL_W), const((1, SMALL_W))],
        out_specs=out_specs,
        out_shape=out_shape,
        scratch_shapes=[pltpu.VMEM((1, SMALL_W), F32)],
        compiler_params=pltpu.CompilerParams(
            dimension_semantics=("arbitrary", "arbitrary"),
            vmem_limit_bytes=VMEM_LIMIT),
    )(x, g, w, cos64, sin64, cos32, sin32, coss, sins, fb)


def _band_body(q_ref, k_ref, vt_ref, base_ref, o_ref, kpad_ref, vtpad_ref, bias_ref):
    b = pl.program_id(0)
    qt = pl.program_id(1)
    S = k_ref.shape[1]

    @pl.when((b == 0) & (qt == 0))
    def _():
        r = lax.broadcasted_iota(jnp.int32, (TQ_A, WIN_A), 0)
        j = lax.broadcasted_iota(jnp.int32, (TQ_A, WIN_A), 1)
        dchunk = r // CHUNK + A_LEFT_CHUNKS - j // CHUNK
        in_band = (dchunk >= 0) & (dchunk <= A_LEFT_CHUNKS)
        for h in range(HEADS):
            base = jnp.broadcast_to(base_ref[h:h + 1, :], (TQ_A, BASE_A))
            toeplitz = pltpu.roll(base, 0, 1, stride=1, stride_axis=0)
            bias_ref[h] = jnp.where(in_band, toeplitz[:, :WIN_A], NEG_INF).T

    @pl.when(qt == 0)
    def _():
        kpad_ref[0:PAD_A, :] = jnp.zeros((PAD_A, GROUP_W), BF16)
        vtpad_ref[:, 0:PAD_A] = jnp.zeros((GROUP_W, PAD_A), BF16)
        kpad_ref[PAD_A:PAD_A + S, :] = k_ref[0]
        vtpad_ref[:, PAD_A:PAD_A + S] = vt_ref[0]

    krow = lax.broadcasted_iota(jnp.int32, (WIN_A, TQ_A), 0)
    for sub in range(SUB_A):
        tile = qt * SUB_A + sub
        start = pl.multiple_of(tile * TQ_A, TQ_A)
        kw = kpad_ref[pl.ds(start, WIN_A), :]
        qm = _masked_queries(q_ref[0, sub * TQ_A:(sub + 1) * TQ_A, :], HEAD_DIM)
        real_key = krow >= PAD_A - tile * TQ_A
        qk = [_dot_nt(_lane_group(kw, h, HEAD_DIM), qm[h]) for h in range(HEADS)]
        outs = []
        for h in range(HEADS):
            st = jnp.where(real_key, qk[h] + bias_ref[h], NEG_INF)
            m = jnp.max(st, axis=0, keepdims=True)
            p = jnp.exp(st - m)
            l = jnp.sum(p, axis=0, keepdims=True)
            vt = vtpad_ref[h * HEAD_DIM:(h + 1) * HEAD_DIM, pl.ds(start, WIN_A)]
            outs.append(_dot(vt, p.astype(BF16)) / l)
        o_ref[0, sub * TQ_A:(sub + 1) * TQ_A, :] = jnp.concatenate(outs, axis=0).T.astype(BF16)


def _band_call(q, k, vt, base):
    B, S, _ = q.shape
    return pl.pallas_call(
        _band_body,
        name="band",
        grid=(B, S // (SUB_A * TQ_A)),
        in_specs=[pl.BlockSpec((1, SUB_A * TQ_A, GROUP_W), lambda b, i: (b, i, 0)),
                  pl.BlockSpec((1, S, GROUP_W), lambda b, i: (b, 0, 0)),
                  pl.BlockSpec((1, GROUP_W, S), lambda b, i: (b, 0, 0)),
                  pl.BlockSpec((HEADS, BASE_A), lambda b, i: (0, 0))],
        out_specs=pl.BlockSpec((1, SUB_A * TQ_A, GROUP_W), lambda b, i: (b, i, 0)),
        out_shape=jax.ShapeDtypeStruct((B, S, GROUP_W), BF16),
        scratch_shapes=[pltpu.VMEM((PAD_A + S, GROUP_W), BF16),
                        pltpu.VMEM((GROUP_W, PAD_A + S), BF16),
                        pltpu.VMEM((HEADS, WIN_A, TQ_A), F32)],
        compiler_params=pltpu.CompilerParams(
            dimension_semantics=("arbitrary", "arbitrary"),
            vmem_limit_bytes=VMEM_LIMIT),
    )(q, k, vt, base)


def _band_bias_base(rel_bias):
    far = rel_bias[:, 2 * REL_CLIP:]
    n_far = PAD_A - REL_CLIP
    near = rel_bias[:, :0:-1]
    return jnp.concatenate([jnp.broadcast_to(far, (HEADS, n_far)), near,
                            jnp.broadcast_to(far, (HEADS, BASE_A - n_far - 2 * REL_CLIP))],
                           axis=1).astype(F32)


def _online_update_t(st, vt, carry, exp_scale=None):
    m, l, acc = carry
    m_new = jnp.maximum(m, jnp.max(st, axis=0, keepdims=True))
    m_safe = jnp.where(m_new == NEG_INF, 0.0, m_new)
    if exp_scale is None:
        alpha = jnp.exp(m - m_safe)
        p = jnp.exp(st - m_safe)
    else:
        alpha = jnp.exp2((m - m_safe) * exp_scale)
        p = jnp.exp2((st - m_safe) * exp_scale)
    l = alpha * l + jnp.sum(p, axis=0, keepdims=True)
    acc = alpha * acc + _dot(vt, p.astype(BF16))
    return m_new, l, acc


def _online_init_t(tq, d):
    return (jnp.full((1, tq), NEG_INF, F32), jnp.zeros((1, tq), F32), jnp.zeros((d, tq), F32))


def _masked_queries(q, width):
    lane = lax.broadcasted_iota(jnp.int32, (q.shape[0], LANES), 1)
    per_group = LANES // width
    out = []
    for u in range(GROUP_W // width):
        grp = q[:, (u // per_group) * LANES:(u // per_group + 1) * LANES]
        out.append(jnp.where((lane // width) == (u % per_group), grp, 0))
    return out


def _lane_group(x, u, width):
    g = u // (LANES // width)
    return x[:, g * LANES:(g + 1) * LANES]


def _sparse_body(q_ref, k_ref, vt_ref, iq_ref, small_q_ref, small_k_ref, o_ref, key_ref, half_ref):
    qt = pl.program_id(1)
    tq = tk = TQ_B
    n_kb = qt + 1

    w_t = small_q_ref[0].T[SM_WI:SM_WI + IDX_HEADS, :] * ((IDX_DIM * IDX_HEADS) ** -0.5)
    iq = iq_ref[0]
    iq_heads = [iq[:, g * IDX_DIM:(g + 1) * IDX_DIM] for g in range(IDX_HEADS)]
    krow = lax.broadcasted_iota(jnp.int32, (tk, tq), 0)
    qcol = lax.broadcasted_iota(jnp.int32, (tk, tq), 1)
    diag_ok = (krow // CHUNK) <= (qcol // CHUNK)

    def block_at(kb):
        return pl.ds(pl.multiple_of(kb * tk, tk), tk)

    def score_block(kb, _):
        ki = small_k_ref[0, block_at(kb), :][:, SM_KI:SM_KI + IDX_DIM].astype(BF16)
        sc = jnp.zeros((tk, tq), F32)
        for g in range(IDX_HEADS):
            sc = sc + w_t[g:g + 1, :] * jnp.maximum(_dot_nt(ki, iq_heads[g]), 0.0)
        sc = jnp.where(sc == 0.0, 0.0, sc)
        sc = jnp.where((kb < qt) | diag_ok, sc, NEG_INF)
        bits = pltpu.bitcast(sc, jnp.int32)
        key_ref[block_at(kb), :] = jnp.where(bits < 0, bits ^ 0x7FFFFFFF, bits)
        return 0

    lax.fori_loop(0, n_kb, score_block, 0)

    def count(pred):
        def body(kb, cnt):
            hit = jnp.where(pred(key_ref[block_at(kb), :]), 1.0, 0.0)
            return cnt + jnp.sum(hit, axis=0, keepdims=True)
        return lax.fori_loop(0, n_kb, body, jnp.zeros((1, tq), F32))

    def count_half(pred):
        def body(kb, acc):
            return acc + jnp.where(pred(half_ref[block_at(kb), :]), jnp.int16(1), jnp.int16(0))
        acc = lax.fori_loop(0, n_kb, body, jnp.zeros((tk, tq), jnp.int16))
        while acc.shape[0] > 16:
            acc = acc[:acc.shape[0] // 2] + acc[acc.shape[0] // 2:]
        return jnp.sum(acc.astype(F32), axis=0, keepdims=True)

    def search_half(above):
        def accept(cand, t):
            c = count_half(lambda blk: blk >= cand.astype(jnp.int16))
            return jnp.where(above + c >= TOP_K, cand, t)
        t0 = accept(jnp.zeros((1, tq), jnp.int32), jnp.full((1, tq), INT16_MIN, jnp.int32))
        return lax.fori_loop(
            0, 15, lambda it, t: accept(t | jnp.left_shift(jnp.int32(1), 14 - it), t), t0)

    def fill_high(kb, _):
        half_ref[block_at(kb), :] = (key_ref[block_at(kb), :] >> 16).astype(jnp.int16)
        return 0

    lax.fori_loop(0, n_kb, fill_high, 0)
    thr_hi = search_half(0.0)
    n_above = count_half(lambda blk: blk > thr_hi.astype(jnp.int16))

    def fill_low(kb, _):
        key = key_ref[block_at(kb), :]
        low = (key & 0xFFFF) - 0x8000
        half_ref[block_at(kb), :] = jnp.where((key >> 16) == thr_hi, low, INT16_MIN).astype(jnp.int16)
        return 0

    lax.fori_loop(0, n_kb, fill_low, 0)
    thr_lo = search_half(n_above)
    thr = (thr_hi << 16) | (thr_lo + 0x8000)
    n_ge = count(lambda blk: blk >= thr)
    n_gt = count(lambda blk: blk > thr)
    short = thr <= NEG_INF_KEY
    thr = jnp.where(short, NEG_INF_KEY + 1, thr)
    need = jnp.where(short, float(TOP_K), TOP_K - n_gt)
    n_tie = jnp.where(short, 0.0, n_ge - n_gt)
    many_ties = jnp.max(n_tie) > 1.5

    def store_mask(kb, sel):
        key_ref[block_at(kb), :] = pltpu.bitcast(jnp.where(sel, -NEG_INF, NEG_INF), jnp.int32)

    @pl.when(jnp.logical_not(many_ties))
    def _():
        def body(kb, _):
            store_mask(kb, key_ref[block_at(kb), :] >= thr)
            return 0
        lax.fori_loop(0, n_kb, body, 0)

    @pl.when(many_ties)
    def _():
        kr = lax.broadcasted_iota(jnp.int32, (tk, tk), 0)
        kc = lax.broadcasted_iota(jnp.int32, (tk, tk), 1)
        earlier = jnp.where(kc < kr, 1.0, 0.0).astype(BF16)

        def body(kb, ties_seen):
            blk = key_ref[block_at(kb), :]
            tie = blk == thr
            tie_f = jnp.where(tie, 1.0, 0.0)
            rank = _dot(earlier, tie_f.astype(BF16)) + ties_seen
            store_mask(kb, (blk > thr) | (tie & (rank < need)))
            return ties_seen + jnp.sum(tie_f, axis=0, keepdims=True)
        lax.fori_loop(0, n_kb, body, jnp.zeros((1, tq), F32))

    qm = _masked_queries(q_ref[0], HEAD_DIM)

    def attend_block(kb, carry):
        cap = pltpu.bitcast(key_ref[block_at(kb), :], F32)
        kblk = k_ref[0, block_at(kb), :]
        qk = [_dot_nt(_lane_group(kblk, h, HEAD_DIM), qm[h]) for h in range(HEADS)]
        new = []
        for h in range(HEADS):
            st = jnp.minimum(qk[h], cap)
            vt = vt_ref[0, h * HEAD_DIM:(h + 1) * HEAD_DIM, block_at(kb)]
            new.append(_online_update_t(st, vt, carry[h]))
        return tuple(new)

    init = tuple(_online_init_t(tq, HEAD_DIM) for _ in range(HEADS))
    heads = lax.fori_loop(0, n_kb, attend_block, init)
    out_t = jnp.concatenate([acc / l for (_, l, acc) in heads], axis=0)
    o_ref[0] = out_t.T.astype(BF16)


def _sparse_call(q, k, vt, iq, small):
    B, S, _ = q.shape
    qspec = lambda w_: pl.BlockSpec((1, TQ_B, w_), lambda b, i: (b, i, 0))
    full = lambda w_: pl.BlockSpec((1, S, w_), lambda b, i: (b, 0, 0))
    return pl.pallas_call(
        _sparse_body,
        name="sparse",
        grid=(B, S // TQ_B),
        in_specs=[qspec(GROUP_W), full(GROUP_W),
                  pl.BlockSpec((1, GROUP_W, S), lambda b, i: (b, 0, 0)),
                  qspec(GROUP_W), qspec(SMALL_W), full(SMALL_W)],
        out_specs=qspec(GROUP_W),
        out_shape=jax.ShapeDtypeStruct((B, S, GROUP_W), BF16),
        scratch_shapes=[pltpu.VMEM((S, TQ_B), jnp.int32), pltpu.VMEM((S, TQ_B), jnp.int16)],
        compiler_params=pltpu.CompilerParams(
            dimension_semantics=("arbitrary", "arbitrary"),
            vmem_limit_bytes=VMEM_LIMIT),
    )(q, k, vt, iq, small, small)


def _forget_body(q_ref, k_ref, vt_ref, cum_ref, o_ref):
    qt = pl.program_id(1)
    tq, tk = TQ_C, TK_C
    per_tile = tq // tk
    qm = _masked_queries(q_ref[0], HEAD_DIM)
    krow = lax.broadcasted_iota(jnp.int32, (tk, tq), 0)
    qcol = lax.broadcasted_iota(jnp.int32, (tk, tq), 1)

    def block(kb, carry, diag):
        at = pl.ds(pl.multiple_of(kb * tk, tk), tk)
        kblk = k_ref[0, at, :]
        cum = cum_ref[0, at, :]
        qk = [_dot_nt(_lane_group(kblk, h, HEAD_DIM), qm[h]) for h in range(HEADS)]
        new = []
        for h in range(HEADS):
            st = qk[h] - cum[:, SM_FG + h:SM_FG + h + 1]
            if diag is not None:
                st = jnp.where(krow + diag * tk <= qcol, st, NEG_INF)
            vt = vt_ref[0, h * HEAD_DIM:(h + 1) * HEAD_DIM, at]
            new.append(_online_update_t(st, vt, carry[h]))
        return tuple(new)

    init = tuple(_online_init_t(tq, HEAD_DIM) for _ in range(HEADS))
    heads = lax.fori_loop(0, qt * per_tile, functools.partial(block, diag=None), init)
    for j in range(per_tile):
        heads = block(qt * per_tile + j, heads, j)
    out_t = jnp.concatenate([acc / l for (_, l, acc) in heads], axis=0)
    o_ref[0] = out_t.T.astype(BF16)


def _forget_call(q, k, vt, cum):
    B, S, _ = q.shape
    return pl.pallas_call(
        _forget_body,
        name="forget",
        grid=(B, S // TQ_C),
        in_specs=[pl.BlockSpec((1, TQ_C, GROUP_W), lambda b, i: (b, i, 0)),
                  pl.BlockSpec((1, S, GROUP_W), lambda b, i: (b, 0, 0)),
                  pl.BlockSpec((1, GROUP_W, S), lambda b, i: (b, 0, 0)),
                  pl.BlockSpec((1, S, SMALL_W), lambda b, i: (b, 0, 0))],
        out_specs=pl.BlockSpec((1, TQ_C, GROUP_W), lambda b, i: (b, i, 0)),
        out_shape=jax.ShapeDtypeStruct((B, S, GROUP_W), BF16),
        compiler_params=pltpu.CompilerParams(
            dimension_semantics=("arbitrary", "arbitrary"),
            vmem_limit_bytes=VMEM_LIMIT),
    )(q, k, vt, cum)


def _diff_body(q_ref, k_ref, vt_ref, lq1_ref, lk1_ref, lq2_ref, lk2_ref, g_ref, o_ref, *, lam_init):
    qt = pl.program_id(1)
    tq, tk = TQ_D, TK_D
    per_tile = tq // tk
    lam = (jnp.exp(jnp.sum(lq1_ref[...] * lk1_ref[...], axis=1, keepdims=True))
           - jnp.exp(jnp.sum(lq2_ref[...] * lk2_ref[...], axis=1, keepdims=True)) + lam_init)
    units = 2 * HEADS
    qm = _masked_queries(q_ref[0], DIFF_DIM)
    krow = lax.broadcasted_iota(jnp.int32, (tk, tq), 0)
    qcol = lax.broadcasted_iota(jnp.int32, (tk, tq), 1)
    exp_scale = DIFF_DIM ** -0.5 * LOG2E

    def block(kb, carry, diag):
        at = pl.ds(pl.multiple_of(kb * tk, tk), tk)
        kblk = k_ref[0, at, :]
        qk = [_dot_nt(_lane_group(kblk, u, DIFF_DIM), qm[u]) for u in range(units)]
        new = []
        for u in range(units):
            h = u // 2
            st = qk[u]
            if diag is not None:
                st = jnp.where((krow + diag * tk) // CHUNK <= qcol // CHUNK, st, NEG_INF)
            vt = vt_ref[0, h * HEAD_DIM:(h + 1) * HEAD_DIM, at]
            new.append(_online_update_t(st, vt, carry[u], exp_scale))
        return tuple(new)

    init = tuple(_online_init_t(tq, HEAD_DIM) for _ in range(units))
    fin = lax.fori_loop(0, qt * per_tile, functools.partial(block, diag=None), init)
    for j in range(per_tile):
        fin = block(qt * per_tile + j, fin, j)
    outs = []
    for h in range(HEADS):
        (_, l1, acc1), (_, l2, acc2) = fin[2 * h], fin[2 * h + 1]
        o = acc1 / l1 - lam * (acc2 / l2)
        o = o * lax.rsqrt(jnp.mean(o * o, axis=0, keepdims=True) + EPS) * g_ref[:, :1]
        outs.append(o * (1.0 - lam_init))
    o_ref[0] = jnp.concatenate(outs, axis=0).T.astype(BF16)


def _diff_call(q, k, vt, lq1, lk1, lq2, lk2, g_col, lam_init):
    B, S, _ = q.shape
    vec = lambda n: pl.BlockSpec((1, n), lambda b, i: (0, 0))
    return pl.pallas_call(
        functools.partial(_diff_body, lam_init=lam_init),
        name="diff",
        grid=(B, S // TQ_D),
        in_specs=[pl.BlockSpec((1, TQ_D, GROUP_W), lambda b, i: (b, i, 0)),
                  pl.BlockSpec((1, S, GROUP_W), lambda b, i: (b, 0, 0)),
                  pl.BlockSpec((1, GROUP_W, S), lambda b, i: (b, 0, 0)),
                  vec(DIFF_DIM), vec(DIFF_DIM), vec(DIFF_DIM), vec(DIFF_DIM),
                  pl.BlockSpec((HEAD_DIM, LANES), lambda b, i: (0, 0))],
        out_specs=pl.BlockSpec((1, TQ_D, GROUP_W), lambda b, i: (b, i, 0)),
        out_shape=jax.ShapeDtypeStruct((B, S, GROUP_W), BF16),
        compiler_params=pltpu.CompilerParams(
            dimension_semantics=("arbitrary", "arbitrary"),
            vmem_limit_bytes=VMEM_LIMIT),
    )(q, k, vt, lq1, lk1, lq2, lk2, g_col)


def _ffn_body(x_ref, ma_ref, mb_ref, mc_ref, md_ref, wo_ref, g2_ref, wg_ref, wu_ref, wd_ref,
              gf_ref, o_ref, *, final):
    mixed = jnp.concatenate([ma_ref[...], mb_ref[...], mc_ref[...], md_ref[...]], axis=1)
    x1 = x_ref[...] + _dot(mixed, wo_ref[...])
    h2 = _rmsnorm(x1, g2_ref[...]).astype(BF16)
    gate = jax.nn.silu(_dot(h2, wg_ref[...]))
    up = _dot(h2, wu_ref[...])
    x2 = x1 + _dot((gate * up).astype(BF16), wd_ref[...])
    if final:
        x2 = _rmsnorm(x2, gf_ref[...])
    o_ref[...] = x2


def _ffn_call(x, mixed, wo, g2, wg, wu, wd, gf, final):
    T = x.shape[0]
    tm = TM_FFN
    tok = lambda w_: pl.BlockSpec((tm, w_), lambda i: (i, 0))
    const = lambda shape: pl.BlockSpec(shape, lambda i: (0, 0), pipeline_mode=pl.Buffered(1))
    return pl.pallas_call(
        functools.partial(_ffn_body, final=final),
        name="ffn",
        grid=(T // tm,),
        in_specs=[tok(D_MODEL)] + [tok(GROUP_W)] * 4 + [
            const((D_MODEL, D_MODEL)), const((1, D_MODEL)), const((D_MODEL, D_FF)),
            const((D_MODEL, D_FF)), const((D_FF, D_MODEL)), const((1, D_MODEL))],
        out_specs=tok(D_MODEL),
        out_shape=jax.ShapeDtypeStruct((T, D_MODEL), F32),
        compiler_params=pltpu.CompilerParams(
            dimension_semantics=("arbitrary",),
            vmem_limit_bytes=VMEM_LIMIT),
    )(x, *mixed, wo, g2, wg, wu, wd, gf)


def _rope_tables(S):
    pos = jnp.arange(S, dtype=F32)[:, None]

    def table(d, reps):
        inv = ROPE_THETA ** (-jnp.arange(0, d, 2, dtype=F32) / d)
        ang = pos * inv[None, :]
        cos, sin = lax.optimization_barrier((jnp.cos(ang), jnp.sin(ang)))
        return (jnp.tile(jnp.concatenate([cos, cos], axis=1), (1, reps)),
                jnp.tile(jnp.concatenate([-sin, sin], axis=1), (1, reps)))

    cos64, sin64 = table(HEAD_DIM, HEADS)
    cos32, sin32 = table(IDX_DIM, GROUP_W // IDX_DIM)
    coss = jnp.ones((S, SMALL_W), F32).at[:, :IDX_DIM].set(cos32[:, :IDX_DIM])
    sins = jnp.zeros((S, SMALL_W), F32).at[:, :IDX_DIM].set(sin32[:, :IDX_DIM])
    return cos64, sin64, cos32, sin32, coss, sins


def _reorder_w_in(w):
    o = np.cumsum([0, 3 * GROUP_W, 3 * GROUP_W, IDX_HEADS * IDX_DIM, IDX_DIM, IDX_HEADS,
                   3 * GROUP_W, HEADS, 3 * GROUP_W])
    wt = w.T
    pad = jnp.zeros((SMALL_W - IDX_DIM - IDX_HEADS - HEADS, w.shape[0]), w.dtype)
    parts = [wt[o[0]:o[1]], wt[o[1]:o[2]], wt[o[2]:o[3]], wt[o[5]:o[6]], wt[o[7]:o[8]],
             wt[o[3]:o[4]], wt[o[4]:o[5]], wt[o[6]:o[7]], pad]
    return jnp.concatenate(parts, axis=0).astype(BF16)


def kernel(x, ln1_g, w_in, rel_bias, forget_b, lam_q1, lam_k1, lam_q2, lam_k2, diff_norm_g,
           w_o, ln2_g, w_gate, w_up, w_down, final_g):
    B, S, D = x.shape
    tabs = _rope_tables(S)
    row = lambda a: a.reshape(1, -1).astype(F32)
    for l in range(DEPTH):
        lam_init = 0.8 - 0.6 * math.exp(-0.3 * l)
        fb = jnp.zeros((1, SMALL_W), F32).at[0, SM_FG:SM_FG + HEADS].set(forget_b[l])
        (aq, ak, avt, bq, bk, bvt, iq, cq, ck, cvt, dq, dk, dvt, small, cum) = _proj_call(
            x, row(ln1_g[l]), _reorder_w_in(w_in[l]), tabs, fb)
        out_a = _band_call(aq, ak, avt, _band_bias_base(rel_bias[l]))
        out_b = _sparse_call(bq, bk, bvt, iq, small)
        out_c = _forget_call(cq, ck, cvt, cum)
        g_col = jnp.broadcast_to(diff_norm_g[l].reshape(-1, 1).astype(F32), (HEAD_DIM, LANES))
        out_d = _diff_call(dq, dk, dvt, row(lam_q1[l]), row(lam_k1[l]), row(lam_q2[l]),
                           row(lam_k2[l]), g_col, lam_init)
        mixed = [o.reshape(B * S, GROUP_W) for o in (out_a, out_b, out_c, out_d)]
        x = _ffn_call(x.reshape(B * S, D), mixed, w_o[l].astype(BF16), row(ln2_g[l]),
                      w_gate[l].astype(BF16), w_up[l].astype(BF16), w_down[l].astype(BF16),
                      row(final_g), final=(l == DEPTH - 1)).reshape(B, S, D)
    return x
```

```python
import functools
import math

import jax
import jax.numpy as jnp
import numpy as np
from jax import lax
from jax.experimental import pallas as pl
from jax.experimental.pallas import tpu as pltpu

D_MODEL = 1024
SEQ = 2048
DEPTH = 2
CHUNK = 64
HEAD_DIM = 64
HEADS = 4
GROUP_W = HEADS * HEAD_DIM
A_LEFT_CHUNKS = 8
REL_CLIP = 128
IDX_HEADS = 8
IDX_DIM = 32
TOP_K = 256
DIFF_DIM = 32
ROPE_THETA = 10000.0
D_FF = 2816
EPS = 1e-6
LANES = 128
NORM_ROWS = 16

OFF_A = 0
OFF_B = 3 * GROUP_W
OFF_IQ = 6 * GROUP_W
OFF_C = 7 * GROUP_W
OFF_D = 10 * GROUP_W
OFF_SMALL = 13 * GROUP_W
SMALL_W = LANES
PROJ_W = OFF_SMALL + SMALL_W
SM_KI = 0
SM_WI = IDX_DIM
SM_FG = IDX_DIM + IDX_HEADS

VMEM_LIMIT = 56 * 1024 * 1024

TM_PROJ = 512
CUM_SUB = 128
TM_FFN = 512
TQ_A = 2 * CHUNK
SUB_A = 2
WIN_A = (A_LEFT_CHUNKS + 2) * CHUNK
PAD_A = A_LEFT_CHUNKS * CHUNK
BASE_A = WIN_A + TQ_A
TQ_B = 256
TQ_C = 512
TK_C = 256
TQ_D = 512
TK_D = 256

F32 = jnp.float32
BF16 = jnp.bfloat16
NEG_INF = float("-inf")
INT16_MIN = -2 ** 15
NEG_INF_KEY = int(np.int32(np.uint32(0xFF800000) ^ np.uint32(0x7FFFFFFF)))
LOG2E = math.log2(math.e)


def _dot(a, b):
    return jnp.dot(a, b, preferred_element_type=F32)


def _dot_nt(a, b):
    return lax.dot_general(a, b, (((1,), (1,)), ((), ())), preferred_element_type=F32)


def _rmsnorm(x, g):
    return x * lax.rsqrt(jnp.mean(x * x, axis=-1, keepdims=True) + EPS) * g


def _rope(x, cos, sin_signed, half):
    n = x.shape[-1]
    lane = lax.broadcasted_iota(jnp.int32, x.shape, 1)
    first = (lane & half) == 0
    swapped = jnp.where(first, pltpu.roll(x, n - half, 1), pltpu.roll(x, half, 1))
    return x * cos + swapped * sin_signed


def _proj_body(x_ref, g_ref, w_abi_ref, w_c_ref, w_d_ref, w_small_ref, cos64_ref, sin64_ref, cos32_ref, sin32_ref,
               coss_ref, sins_ref, fb_ref,
               aq_ref, ak_ref, avt_ref, bq_ref, bk_ref, bvt_ref, iq_ref,
               cq_ref, ck_ref, cvt_ref, dq_ref, dk_ref, dvt_ref, small_ref, cum_ref,
               carry_ref):
    i = pl.program_id(1)
    tm = x_ref.shape[1]
    h = _rmsnorm(x_ref[0], g_ref[...]).astype(BF16)

    def grp(off, j):
        w_ref, base = {OFF_A: (w_abi_ref, 0), OFF_B: (w_abi_ref, OFF_B), OFF_IQ: (w_abi_ref, OFF_IQ),
                       OFF_C: (w_c_ref, 0), OFF_D: (w_d_ref, 0)}[off]
        return _dot_nt(h, w_ref[base + j * GROUP_W: base + (j + 1) * GROUP_W, :])

    scale = HEAD_DIM ** -0.5
    aq_ref[0] = (grp(OFF_A, 0) * scale).astype(BF16)
    ak_ref[0] = grp(OFF_A, 1).astype(BF16)
    avt_ref[0] = grp(OFF_A, 2).T.astype(BF16)

    cos64, sin64 = cos64_ref[...], sin64_ref[...]
    bq_ref[0] = (_rope(grp(OFF_B, 0), cos64, sin64, HEAD_DIM // 2) * scale).astype(BF16)
    bk_ref[0] = _rope(grp(OFF_B, 1), cos64, sin64, HEAD_DIM // 2).astype(BF16)
    bvt_ref[0] = grp(OFF_B, 2).T.astype(BF16)

    cos32, sin32 = cos32_ref[...], sin32_ref[...]
    iq_ref[0] = _rope(grp(OFF_IQ, 0), cos32, sin32, IDX_DIM // 2).astype(BF16)

    cq_ref[0] = (grp(OFF_C, 0) * scale).astype(BF16)
    ck_ref[0] = grp(OFF_C, 1).astype(BF16)
    cvt_ref[0] = grp(OFF_C, 2).T.astype(BF16)

    dq_ref[0] = _rope(grp(OFF_D, 0), cos32, sin32, DIFF_DIM // 2).astype(BF16)
    dk_ref[0] = _rope(grp(OFF_D, 1), cos32, sin32, DIFF_DIM // 2).astype(BF16)
    dvt_ref[0] = grp(OFF_D, 2).T.astype(BF16)

    small = _dot_nt(h, w_small_ref[...])
    small_ref[0] = _rope(small, coss_ref[...], sins_ref[...], IDX_DIM // 2)

    lane = lax.broadcasted_iota(jnp.int32, small.shape, 1)
    is_fg = (lane >= SM_FG) & (lane < SM_FG + HEADS)
    logf = jnp.where(is_fg, jax.nn.log_sigmoid(small + fb_ref[...]), 0.0)

    @pl.when(i == 0)
    def _():
        carry_ref[...] = jnp.zeros_like(carry_ref)

    hi = logf.astype(BF16)
    rest = logf - hi.astype(F32)
    mid = rest.astype(BF16)
    lo = (rest - mid.astype(F32)).astype(BF16)
    row = lax.broadcasted_iota(jnp.int32, (CUM_SUB, CUM_SUB), 0)
    col = lax.broadcasted_iota(jnp.int32, (CUM_SUB, CUM_SUB), 1)
    tri = jnp.where(row >= col, 1.0, 0.0).astype(BF16)
    run = carry_ref[...]
    for j in range(tm // CUM_SUB):
        rows = slice(j * CUM_SUB, (j + 1) * CUM_SUB)
        local = _dot(tri, hi[rows]) + _dot(tri, mid[rows]) + _dot(tri, lo[rows])
        cum_ref[0, rows, :] = local + run
        run = run + local[CUM_SUB - 1:CUM_SUB, :]
    carry_ref[...] = run


def _proj_call(x, g, weights, tabs, fb):
    B, S, _ = x.shape
    tm = TM_PROJ
    nt = S // tm
    cos64, sin64, cos32, sin32, coss, sins = tabs
    tok = lambda w_: pl.BlockSpec((1, tm, w_), lambda b, i: (b, i, 0))
    tok_t = pl.BlockSpec((1, GROUP_W, tm), lambda b, i: (b, 0, i))
    tab = lambda w_: pl.BlockSpec((tm, w_), lambda b, i: (i, 0))
    const = lambda shape: pl.BlockSpec(shape, lambda b, i: (0,) * len(shape),
                                       pipeline_mode=pl.Buffered(1))
    bf = jax.ShapeDtypeStruct((B, S, GROUP_W), BF16)
    bf_t = jax.ShapeDtypeStruct((B, GROUP_W, S), BF16)
    small = jax.ShapeDtypeStruct((B, S, SMALL_W), F32)
    out_shape = [bf, bf, bf_t, bf, bf, bf_t, bf, bf, bf, bf_t, bf, bf, bf_t, small, small]
    g_, t_ = tok(GROUP_W), tok_t
    out_specs = [g_, g_, t_, g_, g_, t_, g_, g_, g_, t_, g_, g_, t_, tok(SMALL_W), tok(SMALL_W)]
    return pl.pallas_call(
        _proj_body,
        name="proj",
        grid=(B, nt),
        in_specs=[tok(D_MODEL), const((1, D_MODEL))] + [const(w.shape) for w in weights] + [
                  tab(GROUP_W), tab(GROUP_W), tab(GROUP_W), tab(GROUP_W),
                  tab(SMALL_W), tab(SMALL_W), const((1, SMALL_W))],
        out_specs=out_specs,
        out_shape=out_shape,
        scratch_shapes=[pltpu.VMEM((1, SMALL_W), F32)],
        compiler_params=pltpu.CompilerParams(
            dimension_semantics=("arbitrary", "arbitrary"),
            vmem_limit_bytes=VMEM_LIMIT),
    )(x, g, *weights, cos64, sin64, cos32, sin32, coss, sins, fb)


def _band_body(q_ref, k_ref, vt_ref, base_ref, o_ref, kpad_ref, vtpad_ref, bias_ref):
    b = pl.program_id(0)
    qt = pl.program_id(1)
    S = k_ref.shape[1]

    @pl.when((b == 0) & (qt == 0))
    def _():
        r = lax.broadcasted_iota(jnp.int32, (TQ_A, WIN_A), 0)
        j = lax.broadcasted_iota(jnp.int32, (TQ_A, WIN_A), 1)
        dchunk = r // CHUNK + A_LEFT_CHUNKS - j // CHUNK
        in_band = (dchunk >= 0) & (dchunk <= A_LEFT_CHUNKS)
        for h in range(HEADS):
            base = jnp.broadcast_to(base_ref[h:h + 1, :], (TQ_A, BASE_A))
            toeplitz = pltpu.roll(base, 0, 1, stride=1, stride_axis=0)
            bias_ref[h] = jnp.where(in_band, toeplitz[:, :WIN_A], NEG_INF).T

    @pl.when(qt == 0)
    def _():
        kpad_ref[0:PAD_A, :] = jnp.zeros((PAD_A, GROUP_W), BF16)
        vtpad_ref[:, 0:PAD_A] = jnp.zeros((GROUP_W, PAD_A), BF16)
        kpad_ref[PAD_A:PAD_A + S, :] = k_ref[0]
        vtpad_ref[:, PAD_A:PAD_A + S] = vt_ref[0]

    krow = lax.broadcasted_iota(jnp.int32, (WIN_A, TQ_A), 0)
    for sub in range(SUB_A):
        tile = qt * SUB_A + sub
        start = pl.multiple_of(tile * TQ_A, TQ_A)
        kw = kpad_ref[pl.ds(start, WIN_A), :]
        qm = _masked_queries(q_ref[0, sub * TQ_A:(sub + 1) * TQ_A, :], HEAD_DIM)
        real_key = krow >= PAD_A - tile * TQ_A
        qk = [_dot_nt(_lane_group(kw, h, HEAD_DIM), qm[h]) for h in range(HEADS)]
        outs = []
        for h in range(HEADS):
            st = jnp.where(real_key, qk[h] + bias_ref[h], NEG_INF)
            m = jnp.max(st, axis=0, keepdims=True)
            p = jnp.exp(st - m)
            l = jnp.sum(p, axis=0, keepdims=True)
            vt = vtpad_ref[h * HEAD_DIM:(h + 1) * HEAD_DIM, pl.ds(start, WIN_A)]
            outs.append(_dot(vt, p.astype(BF16)) / l)
        o_ref[0, sub * TQ_A:(sub + 1) * TQ_A, :] = jnp.concatenate(outs, axis=0).T.astype(BF16)


def _band_call(q, k, vt, base):
    B, S, _ = q.shape
    return pl.pallas_call(
        _band_body,
        name="band",
        grid=(B, S // (SUB_A * TQ_A)),
        in_specs=[pl.BlockSpec((1, SUB_A * TQ_A, GROUP_W), lambda b, i: (b, i, 0)),
                  pl.BlockSpec((1, S, GROUP_W), lambda b, i: (b, 0, 0)),
                  pl.BlockSpec((1, GROUP_W, S), lambda b, i: (b, 0, 0)),
                  pl.BlockSpec((HEADS, BASE_A), lambda b, i: (0, 0))],
        out_specs=pl.BlockSpec((1, SUB_A * TQ_A, GROUP_W), lambda b, i: (b, i, 0)),
        out_shape=jax.ShapeDtypeStruct((B, S, GROUP_W), BF16),
        scratch_shapes=[pltpu.VMEM((PAD_A + S, GROUP_W), BF16),
                        pltpu.VMEM((GROUP_W, PAD_A + S), BF16),
                        pltpu.VMEM((HEADS, WIN_A, TQ_A), F32)],
        compiler_params=pltpu.CompilerParams(
            dimension_semantics=("arbitrary", "arbitrary"),
            vmem_limit_bytes=VMEM_LIMIT),
    )(q, k, vt, base)


def _band_bias_base(rel_bias):
    far = rel_bias[:, 2 * REL_CLIP:]
    n_far = PAD_A - REL_CLIP
    near = rel_bias[:, :0:-1]
    return jnp.concatenate([jnp.broadcast_to(far, (HEADS, n_far)), near,
                            jnp.broadcast_to(far, (HEADS, BASE_A - n_far - 2 * REL_CLIP))],
                           axis=1).astype(F32)


def _with_ones_rows(vt):
    return jnp.concatenate([vt, jnp.ones((NORM_ROWS, vt.shape[1]), BF16)], axis=0)


def _normalised(acc):
    return acc[:HEAD_DIM] / acc[HEAD_DIM:HEAD_DIM + 1]


def _online_update_t(st, vt, carry, exp_scale=None):
    m, acc = carry
    m_new = jnp.maximum(m, jnp.max(st, axis=0, keepdims=True))
    m_safe = jnp.where(m_new == NEG_INF, 0.0, m_new)
    if exp_scale is None:
        alpha = jnp.exp(m - m_safe)
        p = jnp.exp(st - m_safe)
    else:
        alpha = jnp.exp2((m - m_safe) * exp_scale)
        p = jnp.exp2((st - m_safe) * exp_scale)
    acc = alpha * acc + _dot(_with_ones_rows(vt), p.astype(BF16))
    return m_new, acc


def _online_init_t(tq):
    return (jnp.full((1, tq), NEG_INF, F32), jnp.zeros((HEAD_DIM + NORM_ROWS, tq), F32))


def _masked_queries(q, width):
    lane = lax.broadcasted_iota(jnp.int32, (q.shape[0], LANES), 1)
    per_group = LANES // width
    out = []
    for u in range(GROUP_W // width):
        grp = q[:, (u // per_group) * LANES:(u // per_group + 1) * LANES]
        out.append(jnp.where((lane // width) == (u % per_group), grp, 0))
    return out


def _lane_group(x, u, width):
    g = u // (LANES // width)
    return x[:, g * LANES:(g + 1) * LANES]


def _sparse_body(q_ref, k_ref, vt_ref, iq_ref, small_q_ref, small_k_ref, o_ref, key_ref, half_ref):
    qt = pl.program_id(1)
    tq = tk = TQ_B
    n_kb = qt + 1

    w_t = small_q_ref[0].T[SM_WI:SM_WI + IDX_HEADS, :] * ((IDX_DIM * IDX_HEADS) ** -0.5)
    iq = iq_ref[0]
    iq_heads = [iq[:, g * IDX_DIM:(g + 1) * IDX_DIM] for g in range(IDX_HEADS)]
    krow = lax.broadcasted_iota(jnp.int32, (tk, tq), 0)
    qcol = lax.broadcasted_iota(jnp.int32, (tk, tq), 1)
    diag_ok = (krow // CHUNK) <= (qcol // CHUNK)

    def block_at(kb):
        return pl.ds(pl.multiple_of(kb * tk, tk), tk)

    def score_block(kb, _):
        ki = small_k_ref[0, block_at(kb), :][:, SM_KI:SM_KI + IDX_DIM].astype(BF16)
        sc = jnp.zeros((tk, tq), F32)
        for g in range(IDX_HEADS):
            sc = sc + w_t[g:g + 1, :] * jnp.maximum(_dot_nt(ki, iq_heads[g]), 0.0)
        sc = jnp.where(sc == 0.0, 0.0, sc)
        sc = jnp.where((kb < qt) | diag_ok, sc, NEG_INF)
        bits = pltpu.bitcast(sc, jnp.int32)
        key_ref[block_at(kb), :] = jnp.where(bits < 0, bits ^ 0x7FFFFFFF, bits)
        return 0

    def score_pair(i, _):
        score_block(2 * i, 0)
        return score_block(2 * i + 1, 0)

    lax.fori_loop(0, n_kb // 2, score_pair, 0)

    @pl.when(n_kb % 2 == 1)
    def _():
        score_block(n_kb - 1, 0)

    def count(pred):
        def body(kb, cnt):
            hit = jnp.where(pred(key_ref[block_at(kb), :]), 1.0, 0.0)
            return cnt + jnp.sum(hit, axis=0, keepdims=True)
        return lax.fori_loop(0, n_kb, body, jnp.zeros((1, tq), F32))

    def count_half(pred):
        def body(kb, acc):
            return acc + jnp.where(pred(half_ref[block_at(kb), :]), jnp.int16(1), jnp.int16(0))
        acc = lax.fori_loop(0, n_kb, body, jnp.zeros((tk, tq), jnp.int16))
        while acc.shape[0] > 16:
            acc = acc[:acc.shape[0] // 2] + acc[acc.shape[0] // 2:]
        return jnp.sum(acc.astype(F32), axis=0, keepdims=True)

    def search_half(above):
        def accept(cand, t):
            c = count_half(lambda blk: blk >= cand.astype(jnp.int16))
            return jnp.where(above + c >= TOP_K, cand, t)
        t0 = accept(jnp.zeros((1, tq), jnp.int32), jnp.full((1, tq), INT16_MIN, jnp.int32))
        return lax.fori_loop(
            0, 15, lambda it, t: accept(t | jnp.left_shift(jnp.int32(1), 14 - it), t), t0)

    def fill_high(kb, _):
        half_ref[block_at(kb), :] = (key_ref[block_at(kb), :] >> 16).astype(jnp.int16)
        return 0

    lax.fori_loop(0, n_kb, fill_high, 0)
    thr_hi = search_half(0.0)
    n_above = count_half(lambda blk: blk > thr_hi.astype(jnp.int16))

    def fill_low(kb, _):
        key = key_ref[block_at(kb), :]
        low = (key & 0xFFFF) - 0x8000
        half_ref[block_at(kb), :] = jnp.where((key >> 16) == thr_hi, low, INT16_MIN).astype(jnp.int16)
        return 0

    lax.fori_loop(0, n_kb, fill_low, 0)
    thr_lo = search_half(n_above)
    thr = (thr_hi << 16) | (thr_lo + 0x8000)
    n_ge = count(lambda blk: blk >= thr)
    n_gt = count(lambda blk: blk > thr)
    short = thr <= NEG_INF_KEY
    thr = jnp.where(short, NEG_INF_KEY + 1, thr)
    need = jnp.where(short, float(TOP_K), TOP_K - n_gt)
    n_tie = jnp.where(short, 0.0, n_ge - n_gt)
    many_ties = jnp.max(n_tie) > 1.5

    def store_mask(kb, sel):
        key_ref[block_at(kb), :] = pltpu.bitcast(jnp.where(sel, -NEG_INF, NEG_INF), jnp.int32)

    @pl.when(jnp.logical_not(many_ties))
    def _():
        def body(kb, _):
            store_mask(kb, key_ref[block_at(kb), :] >= thr)
            return 0
        lax.fori_loop(0, n_kb, body, 0)

    @pl.when(many_ties)
    def _():
        kr = lax.broadcasted_iota(jnp.int32, (tk, tk), 0)
        kc = lax.broadcasted_iota(jnp.int32, (tk, tk), 1)
        earlier = jnp.where(kc < kr, 1.0, 0.0).astype(BF16)

        def body(kb, ties_seen):
            blk = key_ref[block_at(kb), :]
            tie = blk == thr
            tie_f = jnp.where(tie, 1.0, 0.0)
            rank = _dot(earlier, tie_f.astype(BF16)) + ties_seen
            store_mask(kb, (blk > thr) | (tie & (rank < need)))
            return ties_seen + jnp.sum(tie_f, axis=0, keepdims=True)
        lax.fori_loop(0, n_kb, body, jnp.zeros((1, tq), F32))

    qm = _masked_queries(q_ref[0], HEAD_DIM)

    def attend_block(kb, carry):
        cap = pltpu.bitcast(key_ref[block_at(kb), :], F32)
        kblk = k_ref[0, block_at(kb), :]
        qk = [_dot_nt(_lane_group(kblk, h, HEAD_DIM), qm[h]) for h in range(HEADS)]
        new = []
        for h in range(HEADS):
            st = jnp.minimum(qk[h], cap)
            vt = vt_ref[0, h * HEAD_DIM:(h + 1) * HEAD_DIM, block_at(kb)]
            new.append(_online_update_t(st, vt, carry[h]))
        return tuple(new)

    init = tuple(_online_init_t(tq) for _ in range(HEADS))
    def attend_pair(i, carry):
        return attend_block(2 * i + 1, attend_block(2 * i, carry))

    heads = lax.fori_loop(0, n_kb // 2, attend_pair, init)
    heads = lax.cond(n_kb % 2 == 1, lambda c: attend_block(n_kb - 1, c), lambda c: c, heads)
    out_t = jnp.concatenate([_normalised(acc) for (_, acc) in heads], axis=0)
    o_ref[0] = out_t.T.astype(BF16)


def _sparse_call(q, k, vt, iq, small):
    B, S, _ = q.shape
    qspec = lambda w_: pl.BlockSpec((1, TQ_B, w_), lambda b, i: (b, i, 0))
    full = lambda w_: pl.BlockSpec((1, S, w_), lambda b, i: (b, 0, 0))
    return pl.pallas_call(
        _sparse_body,
        name="sparse",
        grid=(B, S // TQ_B),
        in_specs=[qspec(GROUP_W), full(GROUP_W),
                  pl.BlockSpec((1, GROUP_W, S), lambda b, i: (b, 0, 0)),
                  qspec(GROUP_W), qspec(SMALL_W), full(SMALL_W)],
        out_specs=qspec(GROUP_W),
        out_shape=jax.ShapeDtypeStruct((B, S, GROUP_W), BF16),
        scratch_shapes=[pltpu.VMEM((S, TQ_B), jnp.int32), pltpu.VMEM((S, TQ_B), jnp.int16)],
        compiler_params=pltpu.CompilerParams(
            dimension_semantics=("arbitrary", "arbitrary"),
            vmem_limit_bytes=VMEM_LIMIT),
    )(q, k, vt, iq, small, small)


def _forget_body(q_ref, k_ref, vt_ref, cum_ref, o_ref):
    qt = pl.program_id(1)
    tq, tk = TQ_C, TK_C
    per_tile = tq // tk
    qm = _masked_queries(q_ref[0], HEAD_DIM)
    krow = lax.broadcasted_iota(jnp.int32, (tk, tq), 0)
    qcol = lax.broadcasted_iota(jnp.int32, (tk, tq), 1)

    def block(kb, carry, diag):
        at = pl.ds(pl.multiple_of(kb * tk, tk), tk)
        kblk = k_ref[0, at, :]
        cum = cum_ref[0, at, :]
        qk = [_dot_nt(_lane_group(kblk, h, HEAD_DIM), qm[h]) for h in range(HEADS)]
        new = []
        for h in range(HEADS):
            st = qk[h] - cum[:, SM_FG + h:SM_FG + h + 1]
            if diag is not None:
                st = jnp.where(krow + diag * tk <= qcol, st, NEG_INF)
            vt = vt_ref[0, h * HEAD_DIM:(h + 1) * HEAD_DIM, at]
            new.append(_online_update_t(st, vt, carry[h]))
        return tuple(new)

    init = tuple(_online_init_t(tq) for _ in range(HEADS))
    def past_blocks(i, carry):
        for j in range(per_tile):
            carry = block(i * per_tile + j, carry, None)
        return carry

    heads = lax.fori_loop(0, qt, past_blocks, init)
    for j in range(per_tile):
        heads = block(qt * per_tile + j, heads, j)
    out_t = jnp.concatenate([_normalised(acc) for (_, acc) in heads], axis=0)
    o_ref[0] = out_t.T.astype(BF16)


def _forget_call(q, k, vt, cum):
    B, S, _ = q.shape
    return pl.pallas_call(
        _forget_body,
        name="forget",
        grid=(B, S // TQ_C),
        in_specs=[pl.BlockSpec((1, TQ_C, GROUP_W), lambda b, i: (b, i, 0)),
                  pl.BlockSpec((1, S, GROUP_W), lambda b, i: (b, 0, 0)),
                  pl.BlockSpec((1, GROUP_W, S), lambda b, i: (b, 0, 0)),
                  pl.BlockSpec((1, S, SMALL_W), lambda b, i: (b, 0, 0))],
        out_specs=pl.BlockSpec((1, TQ_C, GROUP_W), lambda b, i: (b, i, 0)),
        out_shape=jax.ShapeDtypeStruct((B, S, GROUP_W), BF16),
        compiler_params=pltpu.CompilerParams(
            dimension_semantics=("arbitrary", "arbitrary"),
            vmem_limit_bytes=VMEM_LIMIT),
    )(q, k, vt, cum)


def _diff_body(q_ref, k_ref, vt_ref, lq1_ref, lk1_ref, lq2_ref, lk2_ref, g_ref, o_ref, *, lam_init):
    qt = pl.program_id(1)
    tq, tk = TQ_D, TK_D
    per_tile = tq // tk
    lam = (jnp.exp(jnp.sum(lq1_ref[...] * lk1_ref[...], axis=1, keepdims=True))
           - jnp.exp(jnp.sum(lq2_ref[...] * lk2_ref[...], axis=1, keepdims=True)) + lam_init)
    units = 2 * HEADS
    qm = _masked_queries(q_ref[0], DIFF_DIM)
    krow = lax.broadcasted_iota(jnp.int32, (tk, tq), 0)
    qcol = lax.broadcasted_iota(jnp.int32, (tk, tq), 1)
    exp_scale = DIFF_DIM ** -0.5 * LOG2E

    def block(kb, carry, diag):
        at = pl.ds(pl.multiple_of(kb * tk, tk), tk)
        kblk = k_ref[0, at, :]
        qk = [_dot_nt(_lane_group(kblk, u, DIFF_DIM), qm[u]) for u in range(units)]
        new = []
        for u in range(units):
            h = u // 2
            st = qk[u]
            if diag is not None:
                st = jnp.where((krow + diag * tk) // CHUNK <= qcol // CHUNK, st, NEG_INF)
            vt = vt_ref[0, h * HEAD_DIM:(h + 1) * HEAD_DIM, at]
            new.append(_online_update_t(st, vt, carry[u], exp_scale))
        return tuple(new)

    init = tuple(_online_init_t(tq) for _ in range(units))
    def past_blocks(i, carry):
        for j in range(per_tile):
            carry = block(i * per_tile + j, carry, None)
        return carry

    fin = lax.fori_loop(0, qt, past_blocks, init)
    for j in range(per_tile):
        fin = block(qt * per_tile + j, fin, j)
    outs = []
    for h in range(HEADS):
        o = _normalised(fin[2 * h][1]) - lam * _normalised(fin[2 * h + 1][1])
        o = o * lax.rsqrt(jnp.mean(o * o, axis=0, keepdims=True) + EPS) * g_ref[:, :1]
        outs.append(o * (1.0 - lam_init))
    o_ref[0] = jnp.concatenate(outs, axis=0).T.astype(BF16)


def _diff_call(q, k, vt, lq1, lk1, lq2, lk2, g_col, lam_init):
    B, S, _ = q.shape
    vec = lambda n: pl.BlockSpec((1, n), lambda b, i: (0, 0))
    return pl.pallas_call(
        functools.partial(_diff_body, lam_init=lam_init),
        name="diff",
        grid=(B, S // TQ_D),
        in_specs=[pl.BlockSpec((1, TQ_D, GROUP_W), lambda b, i: (b, i, 0)),
                  pl.BlockSpec((1, S, GROUP_W), lambda b, i: (b, 0, 0)),
                  pl.BlockSpec((1, GROUP_W, S), lambda b, i: (b, 0, 0)),
                  vec(DIFF_DIM), vec(DIFF_DIM), vec(DIFF_DIM), vec(DIFF_DIM),
                  pl.BlockSpec((HEAD_DIM, LANES), lambda b, i: (0, 0))],
        out_specs=pl.BlockSpec((1, TQ_D, GROUP_W), lambda b, i: (b, i, 0)),
        out_shape=jax.ShapeDtypeStruct((B, S, GROUP_W), BF16),
        compiler_params=pltpu.CompilerParams(
            dimension_semantics=("arbitrary", "arbitrary"),
            vmem_limit_bytes=VMEM_LIMIT),
    )(q, k, vt, lq1, lk1, lq2, lk2, g_col)


def _ffn_body(x_ref, ma_ref, mb_ref, mc_ref, md_ref, wo_ref, g2_ref, wg_ref, wu_ref, wd_ref,
              gf_ref, o_ref, *, final):
    mixed = jnp.concatenate([ma_ref[...], mb_ref[...], mc_ref[...], md_ref[...]], axis=1)
    x1 = x_ref[...] + _dot(mixed, wo_ref[...])
    h2 = _rmsnorm(x1, g2_ref[...]).astype(BF16)
    gate = jax.nn.silu(_dot(h2, wg_ref[...]))
    up = _dot(h2, wu_ref[...])
    x2 = x1 + _dot((gate * up).astype(BF16), wd_ref[...])
    if final:
        x2 = _rmsnorm(x2, gf_ref[...])
    o_ref[...] = x2


def _ffn_call(x, mixed, wo, g2, wg, wu, wd, gf, final):
    T = x.shape[0]
    tm = TM_FFN
    tok = lambda w_: pl.BlockSpec((tm, w_), lambda i: (i, 0))
    const = lambda shape: pl.BlockSpec(shape, lambda i: (0, 0), pipeline_mode=pl.Buffered(1))
    return pl.pallas_call(
        functools.partial(_ffn_body, final=final),
        name="ffn",
        grid=(T // tm,),
        in_specs=[tok(D_MODEL)] + [tok(GROUP_W)] * 4 + [
            const((D_MODEL, D_MODEL)), const((1, D_MODEL)), const((D_MODEL, D_FF)),
            const((D_MODEL, D_FF)), const((D_FF, D_MODEL)), const((1, D_MODEL))],
        out_specs=tok(D_MODEL),
        out_shape=jax.ShapeDtypeStruct((T, D_MODEL), F32),
        compiler_params=pltpu.CompilerParams(
            dimension_semantics=("arbitrary",),
            vmem_limit_bytes=VMEM_LIMIT),
    )(x, *mixed, wo, g2, wg, wu, wd, gf)


def _rope_tables(S):
    pos = jnp.arange(S, dtype=F32)[:, None]

    def table(d, reps):
        inv = ROPE_THETA ** (-jnp.arange(0, d, 2, dtype=F32) / d)
        ang = pos * inv[None, :]
        cos, sin = lax.optimization_barrier((jnp.cos(ang), jnp.sin(ang)))
        return (jnp.tile(jnp.concatenate([cos, cos], axis=1), (1, reps)),
                jnp.tile(jnp.concatenate([-sin, sin], axis=1), (1, reps)))

    cos64, sin64 = table(HEAD_DIM, HEADS)
    cos32, sin32 = table(IDX_DIM, GROUP_W // IDX_DIM)
    coss = jnp.ones((S, SMALL_W), F32).at[:, :IDX_DIM].set(cos32[:, :IDX_DIM])
    sins = jnp.zeros((S, SMALL_W), F32).at[:, :IDX_DIM].set(sin32[:, :IDX_DIM])
    return cos64, sin64, cos32, sin32, coss, sins


def _split_w_in(w):
    o = np.cumsum([0, 3 * GROUP_W, 3 * GROUP_W, IDX_HEADS * IDX_DIM, IDX_DIM, IDX_HEADS,
                   3 * GROUP_W, HEADS, 3 * GROUP_W])
    wt = w.T
    pad = jnp.zeros((SMALL_W - IDX_DIM - IDX_HEADS - HEADS, w.shape[0]), w.dtype)
    small = jnp.concatenate([wt[o[3]:o[4]], wt[o[4]:o[5]], wt[o[6]:o[7]], pad], axis=0)
    return [p.astype(BF16) for p in (wt[o[0]:o[3]], wt[o[5]:o[6]], wt[o[7]:o[8]], small)]


def kernel(x, ln1_g, w_in, rel_bias, forget_b, lam_q1, lam_k1, lam_q2, lam_k2, diff_norm_g,
           w_o, ln2_g, w_gate, w_up, w_down, final_g):
    B, S, D = x.shape
    tabs = _rope_tables(S)
    row = lambda a: a.reshape(1, -1).astype(F32)
    for l in range(DEPTH):
        lam_init = 0.8 - 0.6 * math.exp(-0.3 * l)
        fb = jnp.zeros((1, SMALL_W), F32).at[0, SM_FG:SM_FG + HEADS].set(forget_b[l])
        (aq, ak, avt, bq, bk, bvt, iq, cq, ck, cvt, dq, dk, dvt, small, cum) = _proj_call(
            x, row(ln1_g[l]), _split_w_in(w_in[l]), tabs, fb)
        out_a = _band_call(aq, ak, avt, _band_bias_base(rel_bias[l]))
        out_b = _sparse_call(bq, bk, bvt, iq, small)
        out_c = _forget_call(cq, ck, cvt, cum)
        g_col = jnp.broadcast_to(diff_norm_g[l].reshape(-1, 1).astype(F32), (HEAD_DIM, LANES))
        out_d = _diff_call(dq, dk, dvt, row(lam_q1[l]), row(lam_k1[l]), row(lam_q2[l]),
                           row(lam_k2[l]), g_col, lam_init)
        mixed = [o.reshape(B * S, GROUP_W) for o in (out_a, out_b, out_c, out_d)]
        x = _ffn_call(x.reshape(B * S, D), mixed, w_o[l].astype(BF16), row(ln2_g[l]),
                      w_gate[l].astype(BF16), w_up[l].astype(BF16), w_down[l].astype(BF16),
                      row(final_g), final=(l == DEPTH - 1)).reshape(B, S, D)
    return x
```
